```python
import math
import jax, jax.numpy as jnp
from jax import lax
import numpy as np

D_MODEL = 1024
BATCH = 8
SEQ = 4096
DEPTH = 4

MIX_WIDTH = D_MODEL
ATTN_HEAD_DIM = 128
ATTN_HEADS = (MIX_WIDTH // 2) // ATTN_HEAD_DIM
ATTN_WIDTH = ATTN_HEADS * ATTN_HEAD_DIM
IDX_HEADS = 8
IDX_HEAD_DIM = 64
TOPK_MAX = 256
LRU_WIDTH = MIX_WIDTH - ATTN_WIDTH
LRU_BLOCKS = 8
LRU_BLOCK = LRU_WIDTH // LRU_BLOCKS
CONV_WIDTH = 4
LRU_C = 8.0
D_FF = 4 * D_MODEL
ROPE_THETA = 10000.0
Q_BLOCK = 128
LN_EPS = 1e-5
N_MOD = 6
SPLIT_SIZES = (ATTN_WIDTH, ATTN_WIDTH, ATTN_WIDTH,
               IDX_HEADS * IDX_HEAD_DIM, IDX_HEAD_DIM, IDX_HEADS,
               LRU_WIDTH, LRU_WIDTH)
IN_COLS = sum(SPLIT_SIZES)

kernel_name = "hymba_dsa_rglru_deepnorm_adaln"


def _deepnorm_alpha():
    return (2.0 * DEPTH) ** 0.25


def _deepnorm_beta():
    return (8.0 * DEPTH) ** -0.25


def layer_norm(x, g, b):
    xf = x.astype(jnp.float32)
    mu = jnp.mean(xf, -1, keepdims=True)
    var = jnp.mean(jnp.square(xf - mu), -1, keepdims=True)
    y = (xf - mu) * lax.rsqrt(var + LN_EPS) * g.astype(jnp.float32) + b.astype(jnp.float32)
    return y.astype(x.dtype)


def rms_norm(x, g):
    xf = x.astype(jnp.float32)
    y = xf * lax.rsqrt(jnp.mean(jnp.square(xf), -1, keepdims=True) + LN_EPS) * g.astype(jnp.float32)
    return y.astype(x.dtype)


def rope_tables(positions, dim):
    inv = ROPE_THETA ** (-jnp.arange(0, dim, 2, dtype=jnp.float32) / dim)
    ang = positions.astype(jnp.float32)[..., None] * inv
    return jnp.cos(ang)[:, :, None, :], jnp.sin(ang)[:, :, None, :]


def apply_rope(x, cos, sin):
    x1, x2 = jnp.split(x.astype(jnp.float32), 2, axis=-1)
    y = jnp.concatenate([x1 * cos - x2 * sin, x2 * cos + x1 * sin], axis=-1)
    return y.astype(x.dtype)


def dsa_attention(q, k, v, qi, ki, wi):
    B, L = q.shape[0], q.shape[1]
    topk = min(TOPK_MAX, L // 4)
    qb = min(Q_BLOCK, L)
    nblk = L // qb

    def to_blocks(a):
        return a.reshape((B, nblk, qb) + a.shape[2:]).swapaxes(0, 1)

    t_blocks = jnp.arange(L, dtype=jnp.int32).reshape(nblk, qb)
    key_pos = jnp.arange(L, dtype=jnp.int32)
    scale = ATTN_HEAD_DIM ** -0.5
    ki32 = ki.astype(jnp.float32)

    def block(args):
        q_b, qi_b, wi_b, t_b = args
        causal = key_pos[None, :] <= t_b[:, None]
        rel = jax.nn.relu(jnp.einsum('bthd,bsd->bths', qi_b.astype(jnp.float32), ki32))
        idx_score = jnp.einsum('bths,bth->bts', rel, wi_b.astype(jnp.float32))
        idx_score = jnp.where(causal[None], idx_score, -jnp.inf)
        _, sel = lax.top_k(idx_score, topk)
        valid = sel <= t_b[None, :, None]
        k_sel = jax.vmap(lambda kb, ib: kb[ib])(k, sel)
        v_sel = jax.vmap(lambda vb, ib: vb[ib])(v, sel)
        s = jnp.einsum('bthd,btkhd->bhtk', q_b, k_sel).astype(jnp.float32) * scale
        s = jnp.where(valid[:, None], s, -jnp.inf)
        p = jax.nn.softmax(s, axis=-1).astype(v.dtype)
        return jnp.einsum('bhtk,btkhd->bthd', p, v_sel)

    out = lax.map(block, (to_blocks(q), to_blocks(qi), to_blocks(wi), t_blocks))
    return out.swapaxes(0, 1).reshape(B, L, -1)


def causal_depthwise_conv(x, w, b):
    S = x.shape[1]
    xp = jnp.pad(x, ((0, 0), (CONV_WIDTH - 1, 0), (0, 0)))
    y = b
    for j in range(CONV_WIDTH):
        y = y + xp[:, j:j + S] * w[j]
    return y


def _lin_combine(left, right):
    a1, b1 = left
    a2, b2 = right
    return a1 * a2, a2 * b1 + b2


def rg_lru(xc, w_a, b_a, w_x, b_x, lam):
    B, S, _ = xc.shape
    xf = xc.astype(jnp.float32)
    xb = xf.reshape(B, S, LRU_BLOCKS, LRU_BLOCK)
    r = jax.nn.sigmoid(jnp.einsum('bsnc,ncd->bsnd', xb, w_a.astype(jnp.float32)).reshape(B, S, -1)
                       + b_a.astype(jnp.float32))
    i = jax.nn.sigmoid(jnp.einsum('bsnc,ncd->bsnd', xb, w_x.astype(jnp.float32)).reshape(B, S, -1)
                       + b_x.astype(jnp.float32))
    log_a = -LRU_C * r * jax.nn.softplus(-lam.astype(jnp.float32))
    a = jnp.exp(log_a)
    u = jnp.sqrt(-jnp.expm1(2.0 * log_a)) * (i * xf)
    _, h = lax.associative_scan(_lin_combine, (a, u), axis=1)
    return h.astype(xc.dtype)


def setup_inputs(seed: int = 0) -> dict:
    key = jax.random.key(seed)
    ks = jax.random.split(key, 24)
    f32 = jnp.float32
    beta = _deepnorm_beta()
    L = DEPTH

    def nrm(k, shape, std):
        return jax.random.normal(k, shape, f32) * std

    x = jax.random.normal(ks[0], (BATCH, SEQ, D_MODEL), f32)
    c = jax.random.normal(ks[1], (BATCH, D_MODEL), f32)
    positions = jnp.broadcast_to(jnp.arange(SEQ, dtype=jnp.int32)[None, :], (BATCH, SEQ))

    w_mod = nrm(ks[2], (L, D_MODEL, N_MOD * D_MODEL), 0.1 * D_MODEL ** -0.5)
    b_mod = nrm(ks[3], (L, N_MOD * D_MODEL), 0.01)

    col_scale = np.ones((IN_COLS,), np.float32)
    col_scale[2 * ATTN_WIDTH:3 * ATTN_WIDTH] = beta
    w_in = nrm(ks[4], (L, D_MODEL, IN_COLS), D_MODEL ** -0.5) * jnp.asarray(col_scale)

    conv_w = nrm(ks[5], (L, CONV_WIDTH, LRU_WIDTH), CONV_WIDTH ** -0.5)
    conv_b = nrm(ks[6], (L, LRU_WIDTH), 0.01)
    w_gate_a = nrm(ks[7], (L, LRU_BLOCKS, LRU_BLOCK, LRU_BLOCK), LRU_BLOCK ** -0.5)
    b_gate_a = nrm(ks[8], (L, LRU_WIDTH), 0.01)
    w_gate_x = nrm(ks[9], (L, LRU_BLOCKS, LRU_BLOCK, LRU_BLOCK), LRU_BLOCK ** -0.5)
    b_gate_x = nrm(ks[10], (L, LRU_WIDTH), 0.01)
    u = jax.random.uniform(ks[11], (L, LRU_WIDTH), f32, 0.9, 0.999)
    a0 = u ** (1.0 / LRU_C)
    lru_lambda = jnp.log(a0) - jnp.log1p(-a0)

    g_attn_out = 1.0 + nrm(ks[12], (L, ATTN_WIDTH), 0.02)
    g_lru_out = 1.0 + nrm(ks[13], (L, LRU_WIDTH), 0.02)
    w_out = nrm(ks[14], (L, MIX_WIDTH, D_MODEL), MIX_WIDTH ** -0.5 * beta)
    ln1_g = 1.0 + nrm(ks[15], (L, D_MODEL), 0.02)
    ln1_b = nrm(ks[16], (L, D_MODEL), 0.02)
    w_up = nrm(ks[17], (L, D_MODEL, D_FF), D_MODEL ** -0.5)
    w_down = nrm(ks[18], (L, D_FF, D_MODEL), D_FF ** -0.5 * beta)
    ln2_g = 1.0 + nrm(ks[19], (L, D_MODEL), 0.02)
    ln2_b = nrm(ks[20], (L, D_MODEL), 0.02)

    return {"x": x, "c": c, "positions": positions,
            "w_mod": w_mod, "b_mod": b_mod, "w_in": w_in,
            "conv_w": conv_w, "conv_b": conv_b,
            "w_gate_a": w_gate_a, "b_gate_a": b_gate_a,
            "w_gate_x": w_gate_x, "b_gate_x": b_gate_x, "lru_lambda": lru_lambda,
            "g_attn_out": g_attn_out, "g_lru_out": g_lru_out, "w_out": w_out,
            "ln1_g": ln1_g, "ln1_b": ln1_b, "w_up": w_up, "w_down": w_down,
            "ln2_g": ln2_g, "ln2_b": ln2_b}


def reference(x, c, positions, w_mod, b_mod, w_in, conv_w, conv_b, w_gate_a, b_gate_a,
              w_gate_x, b_gate_x, lru_lambda, g_attn_out, g_lru_out, w_out,
              ln1_g, ln1_b, w_up, w_down, ln2_g, ln2_b):
    B, S, _ = x.shape
    alpha = _deepnorm_alpha()
    split_at = [int(v) for v in np.cumsum(SPLIT_SIZES)[:-1]]
    cos_a, sin_a = rope_tables(positions, ATTN_HEAD_DIM)
    cos_i, sin_i = rope_tables(positions, IDX_HEAD_DIM)
    c_act = jax.nn.silu(c)

    for l in range(DEPTH):
        mod = (c_act @ w_mod[l] + b_mod[l])[:, None, :]
        sh1, sc1, g1, sh2, sc2, g2 = jnp.split(mod, N_MOD, axis=-1)

        h = x * (1.0 + sc1) + sh1
        p = h @ w_in[l]
        q, k, v, qi, ki, wi, xr, yg = jnp.split(p, split_at, axis=-1)
        q = apply_rope(q.reshape(B, S, ATTN_HEADS, ATTN_HEAD_DIM), cos_a, sin_a)
        k = apply_rope(k.reshape(B, S, ATTN_HEADS, ATTN_HEAD_DIM), cos_a, sin_a)
        v = v.reshape(B, S, ATTN_HEADS, ATTN_HEAD_DIM)
        qi = apply_rope(qi.reshape(B, S, IDX_HEADS, IDX_HEAD_DIM), cos_i, sin_i)
        ki = apply_rope(ki[:, :, None, :], cos_i, sin_i)[:, :, 0, :]
        wi = wi * (IDX_HEADS ** -0.5)
        attn_out = dsa_attention(q, k, v, qi, ki, wi)

        xc = causal_depthwise_conv(xr, conv_w[l], conv_b[l])
        lru_out = rg_lru(xc, w_gate_a[l], b_gate_a[l], w_gate_x[l], b_gate_x[l], lru_lambda[l])
        lru_out = lru_out * jax.nn.gelu(yg, approximate=True)

        merged = jnp.concatenate([rms_norm(attn_out, g_attn_out[l]),
                                  rms_norm(lru_out, g_lru_out[l])], axis=-1)
        mix = merged @ w_out[l]
        x = layer_norm(alpha * x + (1.0 + g1) * mix, ln1_g[l], ln1_b[l])

        h2 = x * (1.0 + sc2) + sh2
        ff = jnp.square(jax.nn.relu(h2 @ w_up[l])) @ w_down[l]
        x = layer_norm(alpha * x + (1.0 + g2) * ff, ln2_g[l], ln2_b[l])
    return x
```

```python
import functools
import math

import jax
import jax.numpy as jnp
import numpy as np
from jax import lax
from jax.experimental import pallas as pl
from jax.experimental.pallas import tpu as pltpu

ATTN_HEAD_DIM = 128
ATTN_HEADS = 4
ATTN_WIDTH = ATTN_HEADS * ATTN_HEAD_DIM
IDX_HEADS = 8
IDX_HEAD_DIM = 64
IDX_WIDTH = IDX_HEADS * IDX_HEAD_DIM
TOPK_MAX = 256
LRU_WIDTH = 512
LRU_BLOCKS = 8
LRU_BLOCK = LRU_WIDTH // LRU_BLOCKS
CONV_WIDTH = 4
LRU_C = 8.0
ROPE_THETA = 10000.0
LN_EPS = 1e-5
N_MOD = 6

LANES = 128
SUBLANES = 8
VMEM_LIMIT_BYTES = 56 * 1024 * 1024

KEY_CHUNK = 256
ROW_TILE = 512
IDX_SUB = 128

SEG_Q, SEG_K, SEG_V, SEG_QI, SEG_XR, SEG_YG, SEG_KIW = 0, 512, 1024, 1536, 2048, 2560, 3072
IN_COLS_PAD = 3200

BF16 = jnp.bfloat16
F32 = jnp.float32
INT_MIN = -2 ** 31
KEY_NEG_INF = int(np.int32(np.uint32(0xFF800000) ^ np.uint32(0x7FFFFFFF)))


def _cparams(n_grid):
    return pltpu.CompilerParams(
        dimension_semantics=("arbitrary",) * n_grid,
        vmem_limit_bytes=VMEM_LIMIT_BYTES)


def _mod_body(c_ref, w_ref, b_ref, o_ref):
    c = c_ref[...]
    ca = (c * jax.nn.sigmoid(c)).astype(BF16)
    o_ref[0] = jnp.dot(ca, w_ref[0].astype(BF16), preferred_element_type=F32) + b_ref[0]


def _mod_call(c, w_mod, b_mod):
    L, D, N = w_mod.shape
    B = c.shape[0]
    tn = 2048 if N % 2048 == 0 else N
    return pl.pallas_call(
        _mod_body,
        grid=(L, N // tn),
        in_specs=[pl.BlockSpec((B, D), lambda l, j: (0, 0)),
                  pl.BlockSpec((1, D, tn), lambda l, j: (l, 0, j)),
                  pl.BlockSpec((1, 1, tn), lambda l, j: (l, 0, j))],
        out_specs=pl.BlockSpec((1, B, tn), lambda l, j: (l, 0, j)),
        out_shape=jax.ShapeDtypeStruct((L, B, N), F32),
        compiler_params=_cparams(2),
        name="adaln_mod",
    )(c, w_mod, b_mod.reshape(L, 1, N))


def _rope_body(pos_ref, inv_a_ref, sgn_a_ref, inv_i_ref, sgn_i_ref, ca_ref, sa_ref, ci_ref, si_ref):
    pos = pos_ref[0].astype(F32)
    ang_a = inv_a_ref[...] * pos
    ang_i = inv_i_ref[...] * pos
    ca_ref[0] = jnp.cos(ang_a).T
    sa_ref[0] = (jnp.sin(ang_a) * sgn_a_ref[...]).T
    ci_ref[0] = jnp.cos(ang_i).T
    si_ref[0] = (jnp.sin(ang_i) * sgn_i_ref[...]).T


def _rope_call(positions):
    B, S = positions.shape
    ts = min(ROW_TILE, S)
    half_a, half_i = ATTN_HEAD_DIM // 2, IDX_HEAD_DIM // 2
    inv_a = ROPE_THETA ** (-jnp.arange(0, ATTN_HEAD_DIM, 2, dtype=F32) / ATTN_HEAD_DIM)
    inv_i = ROPE_THETA ** (-jnp.arange(0, IDX_HEAD_DIM, 2, dtype=F32) / IDX_HEAD_DIM)
    inv_a = jnp.tile(inv_a, LANES // half_a).reshape(LANES, 1)
    inv_i = jnp.tile(inv_i, LANES // half_i).reshape(LANES, 1)
    lane = np.arange(LANES)
    sgn_a = jnp.asarray(np.where(lane % ATTN_HEAD_DIM < half_a, -1.0, 1.0).astype(np.float32)).reshape(LANES, 1)
    sgn_i = jnp.asarray(np.where(lane % IDX_HEAD_DIM < half_i, -1.0, 1.0).astype(np.float32)).reshape(LANES, 1)
    col = pl.BlockSpec((LANES, 1), lambda b, i: (0, 0))
    tab = pl.BlockSpec((1, ts, LANES), lambda b, i: (b, i, 0))
    shp = jax.ShapeDtypeStruct((B, S, LANES), F32)
    return pl.pallas_call(
        _rope_body,
        grid=(B, S // ts),
        in_specs=[pl.BlockSpec((1, 1, ts), lambda b, i: (b, 0, i)), col, col, col, col],
        out_specs=[tab, tab, tab, tab],
        out_shape=[shp, shp, shp, shp],
        compiler_params=_cparams(2),
        name="rope_tables",
    )(positions.reshape(B, 1, S), inv_a, sgn_a, inv_i, sgn_i)


def _rope_a(x, cos, sin_signed):
    return x * cos + pltpu.roll(x, ATTN_HEAD_DIM // 2, 1) * sin_signed


def _rope_i(x, cos, sin_signed, lo_half):
    half = IDX_HEAD_DIM // 2
    partner = jnp.where(lo_half, pltpu.roll(x, LANES - half, 1), pltpu.roll(x, half, 1))
    return x * cos + partner * sin_signed


def _inproj_body(x_ref, sh_ref, sc_ref, w_ref, ca_ref, sa_ref, ci_ref, si_ref,
                 qT_ref, k_ref, vT_ref, qiT_ref, kia_ref, kib_ref, wiT_ref, xr_ref, yg_ref):
    tm = x_ref.shape[1]
    n_chunk = tm // KEY_CHUNK
    h = (x_ref[0] * (1.0 + sc_ref[0]) + sh_ref[0]).astype(BF16)
    ca, sa, ci, si = ca_ref[0], sa_ref[0], ci_ref[0], si_ref[0]
    lane = lax.broadcasted_iota(jnp.int32, (tm, LANES), 1)
    lo_half = (lane % IDX_HEAD_DIM) < (IDX_HEAD_DIM // 2)
    q_scale = ATTN_HEAD_DIM ** -0.5

    def seg(start, width=LANES):
        return jnp.dot(h, w_ref[:, start:start + width], preferred_element_type=F32)

    def store_T(ref, g, val):
        vt = val.T.astype(BF16)
        for c in range(n_chunk):
            ref[0, c, g * LANES:(g + 1) * LANES, :] = vt[:, c * KEY_CHUNK:(c + 1) * KEY_CHUNK]

    for g in range(ATTN_WIDTH // LANES):
        store_T(qT_ref, g, _rope_a(seg(SEG_Q + g * LANES), ca, sa) * q_scale)
        k_ref[0, :, g * LANES:(g + 1) * LANES] = _rope_a(seg(SEG_K + g * LANES), ca, sa).astype(BF16)
        store_T(vT_ref, g, seg(SEG_V + g * LANES))
        store_T(qiT_ref, g, _rope_i(seg(SEG_QI + g * LANES), ci, si, lo_half))
    xr_ref[0] = seg(SEG_XR, LRU_WIDTH)
    yg_ref[0] = seg(SEG_YG, LRU_WIDTH)
    kiw = seg(SEG_KIW)
    ki = jnp.where(lane < IDX_HEAD_DIM, _rope_i(kiw, ci, si, lo_half), 0.0)
    kia_ref[0] = ki.astype(BF16)
    kib_ref[0] = pltpu.roll(ki, IDX_HEAD_DIM, 1).astype(BF16)
    wiT = kiw.T[IDX_HEAD_DIM:IDX_HEAD_DIM + IDX_HEADS, :] * (IDX_HEADS ** -0.5)
    for c in range(n_chunk):
        wiT_ref[0, c] = wiT[:, c * KEY_CHUNK:(c + 1) * KEY_CHUNK]


def _inproj_call(x, mod3, w_in_p, tabs):
    B, S, D = x.shape
    tm = min(ROW_TILE, S)
    nc = S // KEY_CHUNK
    cpb = tm // KEY_CHUNK
    ca, sa, ci, si = tabs
    tab = pl.BlockSpec((1, tm, LANES), lambda b, i: (b, i, 0))
    tspec = pl.BlockSpec((1, cpb, ATTN_WIDTH, KEY_CHUNK), lambda b, i: (b, i, 0, 0))
    tshape = jax.ShapeDtypeStruct((B, nc, ATTN_WIDTH, KEY_CHUNK), BF16)
    rows = lambda w, dt: (pl.BlockSpec((1, tm, w), lambda b, i: (b, i, 0)), jax.ShapeDtypeStruct((B, S, w), dt))
    k_spec, k_shape = rows(ATTN_WIDTH, BF16)
    ki_spec, ki_shape = rows(LANES, BF16)
    r_spec, r_shape = rows(LRU_WIDTH, F32)
    return pl.pallas_call(
        _inproj_body,
        grid=(B, S // tm),
        in_specs=[pl.BlockSpec((1, tm, D), lambda b, i: (b, i, 0)),
                  pl.BlockSpec((1, 1, D), lambda b, i: (b, 0, 0)),
                  pl.BlockSpec((1, 1, D), lambda b, i: (b, 0, 1)),
                  pl.BlockSpec((D, IN_COLS_PAD), lambda b, i: (0, 0), pipeline_mode=pl.Buffered(1)),
                  tab, tab, tab, tab],
        out_specs=[tspec, k_spec, tspec, tspec, ki_spec, ki_spec,
                   pl.BlockSpec((1, cpb, IDX_HEADS, KEY_CHUNK), lambda b, i: (b, i, 0, 0)),
                   r_spec, r_spec],
        out_shape=[tshape, k_shape, tshape, tshape, ki_shape, ki_shape,
                   jax.ShapeDtypeStruct((B, nc, IDX_HEADS, KEY_CHUNK), F32),
                   r_shape, r_shape],
        compiler_params=_cparams(2),
        name="in_proj",
    )(x, mod3, mod3, w_in_p, ca, sa, ci, si)


def _dsa_body(qT_ref, qiT_ref, wiT_ref, k_ref, vT_ref, kia_ref, kib_ref, g_ref, o_ref,
              key_s, bias_s, m_s, l_s, acc_s, *, topk, idx_bits):
    i = pl.program_id(1)
    tq = KEY_CHUNK
    tk = KEY_CHUNK
    n_chunks = i + 1
    q_pos = i * tq + lax.broadcasted_iota(jnp.int32, (1, tq), 1)

    def score_chunk(c, carry):
        for sub in range(tk // IDX_SUB):
            r0 = pl.multiple_of(c * tk + sub * IDX_SUB, IDX_SUB)
            ka = kia_ref[0, pl.ds(r0, IDX_SUB), :]
            kb = kib_ref[0, pl.ds(r0, IDX_SUB), :]
            acc = jnp.zeros((IDX_SUB, tq), F32)
            for p in range(IDX_HEADS // 2):
                qp = qiT_ref[0, 0, p * LANES:(p + 1) * LANES, :]
                sa = jnp.dot(ka, qp, preferred_element_type=F32)
                sb = jnp.dot(kb, qp, preferred_element_type=F32)
                acc = acc + wiT_ref[0, 0, 2 * p:2 * p + 1, :] * jnp.maximum(sa, 0.0)
                acc = acc + wiT_ref[0, 0, 2 * p + 1:2 * p + 2, :] * jnp.maximum(sb, 0.0)
            acc = acc + 0.0
            bits = pltpu.bitcast(acc, jnp.int32)
            mono = bits ^ ((bits >> 31) & 0x7FFFFFFF)
            k_pos = r0 + lax.broadcasted_iota(jnp.int32, (IDX_SUB, 1), 0)
            key_s[pl.ds(r0, IDX_SUB), :] = jnp.where(k_pos <= q_pos, mono, KEY_NEG_INF)
        return carry

    lax.fori_loop(0, n_chunks, score_chunk, 0)

    def count_ge(thr_fn):
        def chunk(c, part):
            blk = key_s[pl.ds(pl.multiple_of(c * tk, tk), tk), :]
            hit = jnp.where(thr_fn(blk, c), 1, 0)
            return part + hit.reshape(tk // SUBLANES, SUBLANES, tq).sum(axis=0)
        part = lax.fori_loop(0, n_chunks, chunk, jnp.zeros((SUBLANES, tq), jnp.int32))
        return part.sum(axis=0, keepdims=True)

    def bit_step(j, carry):
        thr, cnt_thr = carry
        cand = thr ^ jnp.left_shift(jnp.int32(1), 31 - j)
        cnt = count_ge(lambda blk, c: blk >= cand)
        ok = cnt >= topk
        return jnp.where(ok, cand, thr), jnp.where(ok, cnt, cnt_thr)

    thr0 = jnp.full((1, tq), INT_MIN, jnp.int32)
    cnt0 = jnp.full((1, tq), 1, jnp.int32) * (n_chunks * tk)
    thr, cnt_thr = lax.fori_loop(0, 32, bit_step, (thr0, cnt0))

    @pl.when(jnp.max(cnt_thr) > topk)
    def _():
        need = topk - count_ge(lambda blk, c: blk > thr)
        def row_idx(c):
            return c * tk + lax.broadcasted_iota(jnp.int32, (tk, 1), 0)
        def idx_step(j, cut):
            cand = cut | jnp.left_shift(jnp.int32(1), idx_bits - 1 - j)
            cnt = count_ge(lambda blk, c: (blk == thr) & (row_idx(c) < cand))
            return jnp.where(cnt < need, cand, cut)
        cut = lax.fori_loop(0, idx_bits, idx_step, jnp.zeros((1, tq), jnp.int32))
        fix = cnt_thr > topk
        def rewrite(c, carry):
            sl = pl.ds(pl.multiple_of(c * tk, tk), tk)
            blk = key_s[sl, :]
            drop = (blk == thr) & (row_idx(c) > cut) & fix
            key_s[sl, :] = jnp.where(drop, blk - 1, blk)
            return carry
        lax.fori_loop(0, n_chunks, rewrite, 0)

    m_s[...] = jnp.full(m_s.shape, -jnp.inf, F32)
    l_s[...] = jnp.zeros(l_s.shape, F32)
    acc_s[...] = jnp.zeros(acc_s.shape, F32)

    def attn_chunk(c, carry):
        r0 = pl.multiple_of(c * tk, tk)
        k_pos = r0 + lax.broadcasted_iota(jnp.int32, (tk, 1), 0)
        sel = (key_s[pl.ds(r0, tk), :] >= thr) & (k_pos <= q_pos)
        bias_s[...] = jnp.where(sel, 0.0, -jnp.inf)
        for hd in range(ATTN_HEADS):
            hs = slice(hd * ATTN_HEAD_DIM, (hd + 1) * ATTN_HEAD_DIM)
            s = jnp.dot(k_ref[0, pl.ds(r0, tk), hs], qT_ref[0, 0, hs, :],
                        preferred_element_type=F32) + bias_s[...]
            m_old = m_s[hd]
            m_new = jnp.maximum(m_old, jnp.max(s, axis=0, keepdims=True))
            m_safe = jnp.where(m_new == -jnp.inf, 0.0, m_new)
            alpha = jnp.exp(m_old - m_safe)
            p = jnp.exp(s - m_safe)
            l_s[hd] = alpha * l_s[hd] + jnp.sum(p, axis=0, keepdims=True)
            acc_s[hd] = alpha * acc_s[hd] + jnp.dot(vT_ref[0, c, hs, :], p.astype(BF16),
                                                    preferred_element_type=F32)
            m_s[hd] = m_new
        return carry

    lax.fori_loop(0, n_chunks, attn_chunk, 0)

    outT = jnp.concatenate([acc_s[hd] / l_s[hd] for hd in range(ATTN_HEADS)], axis=0)
    out = outT.T
    ms = jnp.mean(jnp.square(out), axis=-1, keepdims=True)
    o_ref[0] = (out * lax.rsqrt(ms + LN_EPS) * g_ref[...]).astype(BF16)


def _dsa_call(qT, qiT, wiT, k, vT, kia, kib, g_attn):
    B, S, _ = k.shape
    nc = S // KEY_CHUNK
    topk = min(TOPK_MAX, S // 4)
    idx_bits = max(1, int(math.ceil(math.log2(S))))
    blk_q = pl.BlockSpec((1, 1, ATTN_WIDTH, KEY_CHUNK), lambda b, i: (b, i, 0, 0))
    full = lambda shape: pl.BlockSpec(shape, lambda b, i: (b,) + (0,) * (len(shape) - 1))
    return pl.pallas_call(
        functools.partial(_dsa_body, topk=topk, idx_bits=idx_bits),
        grid=(B, nc),
        in_specs=[blk_q, blk_q,
                  pl.BlockSpec((1, 1, IDX_HEADS, KEY_CHUNK), lambda b, i: (b, i, 0, 0)),
                  full((1, S, ATTN_WIDTH)),
                  full((1, nc, ATTN_WIDTH, KEY_CHUNK)),
                  full((1, S, LANES)), full((1, S, LANES)),
                  pl.BlockSpec((1, ATTN_WIDTH), lambda b, i: (0, 0))],
        out_specs=pl.BlockSpec((1, KEY_CHUNK, ATTN_WIDTH), lambda b, i: (b, i, 0)),
        out_shape=jax.ShapeDtypeStruct((B, S, ATTN_WIDTH), BF16),
        scratch_shapes=[pltpu.VMEM((S, KEY_CHUNK), jnp.int32),
                        pltpu.VMEM((KEY_CHUNK, KEY_CHUNK), F32),
                        pltpu.VMEM((ATTN_HEADS, 1, KEY_CHUNK), F32),
                        pltpu.VMEM((ATTN_HEADS, 1, KEY_CHUNK), F32),
                        pltpu.VMEM((ATTN_HEADS, ATTN_HEAD_DIM, KEY_CHUNK), F32)],
        compiler_params=_cparams(2),
        name="dsa_attention",
    )(qT, qiT, wiT, k, vT, kia, kib, g_attn.reshape(1, ATTN_WIDTH))


def _lru_body(xr_ref, yg_ref, cw_ref, cb_ref, wa_ref, wx_ref, ba_ref, bx_ref, lam_ref, g_ref, o_ref,
              xbuf_s, a_s, u_s, h_s, carry_s):
    ts = xr_ref.shape[1]
    W = xr_ref.shape[2]
    pad = SUBLANES

    @pl.when(pl.program_id(1) == 0)
    def _():
        xbuf_s[0:pad, :] = jnp.zeros((pad, W), F32)
        carry_s[...] = jnp.zeros(carry_s.shape, F32)

    xbuf_s[pad:pad + ts, :] = xr_ref[0]
    xc = cb_ref[...]
    for j in range(CONV_WIDTH):
        xc = xc + xbuf_s[pad - (CONV_WIDTH - 1) + j:pad - (CONV_WIDTH - 1) + j + ts, :] * cw_ref[j:j + 1, :]
    xbuf_s[0:pad, :] = xbuf_s[ts:ts + pad, :]

    xcb = xc.astype(BF16)
    r = jax.nn.sigmoid(jnp.dot(xcb, wa_ref[...], preferred_element_type=F32) + ba_ref[...])
    gi = jax.nn.sigmoid(jnp.dot(xcb, wx_ref[...], preferred_element_type=F32) + bx_ref[...])
    neg_lam = -lam_ref[...]
    softplus = jnp.maximum(neg_lam, 0.0) + jnp.log1p(jnp.exp(-jnp.abs(neg_lam)))
    log_a = -LRU_C * r * softplus
    a = jnp.exp(log_a)
    u = jnp.sqrt(-jnp.tanh(log_a) * (a * a + 1.0)) * (gi * xc)

    row = lax.broadcasted_iota(jnp.int32, (ts, 1), 0) % SUBLANES
    d = 1
    while d < SUBLANES:
        keep = row >= d
        a_sh = jnp.where(keep, pltpu.roll(a, d, 0), 1.0)
        u_sh = jnp.where(keep, pltpu.roll(u, d, 0), 0.0)
        u = a * u_sh + u
        a = a * a_sh
        d *= 2
    a_s[...] = a
    u_s[...] = u

    def group(g, hprev):
        sl = pl.ds(pl.multiple_of(g * SUBLANES, SUBLANES), SUBLANES)
        hg = a_s[sl, :] * hprev + u_s[sl, :]
        h_s[sl, :] = hg
        return hg[SUBLANES - 1:SUBLANES, :]

    carry_s[...] = lax.fori_loop(0, ts // SUBLANES, group, carry_s[...], unroll=8)

    y = h_s[...] * jax.nn.gelu(yg_ref[0], approximate=True)
    ms = jnp.mean(jnp.square(y), axis=-1, keepdims=True)
    o_ref[0] = (y * lax.rsqrt(ms + LN_EPS) * g_ref[...]).astype(BF16)


def _lru_call(xr, yg, conv_w, conv_b, wa_bd, wx_bd, b_a, b_x, lam, g_lru):
    B, S, W = xr.shape
    ts = min(ROW_TILE, S)
    rows = pl.BlockSpec((1, ts, W), lambda b, i: (b, i, 0))
    vec = pl.BlockSpec((1, W), lambda b, i: (0, 0))
    mat = pl.BlockSpec((W, W), lambda b, i: (0, 0))
    return pl.pallas_call(
        _lru_body,
        grid=(B, S // ts),
        in_specs=[rows, rows, pl.BlockSpec((CONV_WIDTH, W), lambda b, i: (0, 0)), vec, mat, mat,
                  vec, vec, vec, vec],
        out_specs=rows,
        out_shape=jax.ShapeDtypeStruct((B, S, W), BF16),
        scratch_shapes=[pltpu.VMEM((ts + SUBLANES, W), F32),
                        pltpu.VMEM((ts, W), F32), pltpu.VMEM((ts, W), F32), pltpu.VMEM((ts, W), F32),
                        pltpu.VMEM((1, W), F32)],
        compiler_params=_cparams(2),
        name="rg_lru",
    )(xr, yg, conv_w, conv_b.reshape(1, W), wa_bd, wx_bd, b_a.reshape(1, W), b_x.reshape(1, W),
      lam.reshape(1, W), g_lru.reshape(1, W))


def _layer_norm(y, g, b):
    mu = jnp.mean(y, axis=-1, keepdims=True)
    yc = y - mu
    var = jnp.mean(jnp.square(yc), axis=-1, keepdims=True)
    return yc * lax.rsqrt(var + LN_EPS) * g + b


def _mlp_body(at_ref, lr_ref, x_ref, g1_ref, sh2_ref, sc2_ref, g2_ref, wo_ref, l1g_ref, l1b_ref,
              wu_ref, wd_ref, l2g_ref, l2b_ref, o_ref, *, alpha, ff_chunk):
    aw = at_ref.shape[2]
    mix = jnp.dot(at_ref[0], wo_ref[0:aw, :], preferred_element_type=F32)
    mix = mix + jnp.dot(lr_ref[0], wo_ref[aw:, :], preferred_element_type=F32)
    x1 = _layer_norm(alpha * x_ref[0] + (1.0 + g1_ref[0]) * mix, l1g_ref[...], l1b_ref[...])
    h2 = (x1 * (1.0 + sc2_ref[0]) + sh2_ref[0]).astype(BF16)
    d_ff = wu_ref.shape[1]
    ff = jnp.zeros(x1.shape, F32)
    for c in range(d_ff // ff_chunk):
        cs = slice(c * ff_chunk, (c + 1) * ff_chunk)
        up = jnp.dot(h2, wu_ref[:, cs], preferred_element_type=F32)
        act = jnp.square(jnp.maximum(up, 0.0)).astype(BF16)
        ff = ff + jnp.dot(act, wd_ref[cs, :], preferred_element_type=F32)
    o_ref[0] = _layer_norm(alpha * x1 + (1.0 + g2_ref[0]) * ff, l2g_ref[...], l2b_ref[...])


def _mlp_call(attn_n, lru_n, x, mod3, w_out, ln1_g, ln1_b, w_up, w_down, ln2_g, ln2_b, alpha):
    B, S, D = x.shape
    tm = min(ROW_TILE, S)
    d_ff = w_up.shape[1]
    half = pl.BlockSpec((1, tm, attn_n.shape[2]), lambda b, i: (b, i, 0))
    rows = pl.BlockSpec((1, tm, D), lambda b, i: (b, i, 0))
    modv = lambda j: pl.BlockSpec((1, 1, D), lambda b, i: (b, 0, j))
    vec = pl.BlockSpec((1, D), lambda b, i: (0, 0))
    wspec = lambda shape: pl.BlockSpec(shape, lambda b, i: (0, 0), pipeline_mode=pl.Buffered(1))
    return pl.pallas_call(
        functools.partial(_mlp_body, alpha=alpha, ff_chunk=min(1024, d_ff)),
        grid=(B, S // tm),
        in_specs=[half, half, rows, modv(2), modv(3), modv(4), modv(5),
                  wspec((D, D)), vec, vec, wspec((D, d_ff)), wspec((d_ff, D)), vec, vec],
        out_specs=rows,
        out_shape=jax.ShapeDtypeStruct((B, S, D), F32),
        compiler_params=_cparams(2),
        name="out_mlp",
    )(attn_n, lru_n, x, mod3, mod3, mod3, mod3, w_out, ln1_g.reshape(1, D), ln1_b.reshape(1, D),
      w_up, w_down, ln2_g.reshape(1, D), ln2_b.reshape(1, D))


def _prep_w_in(w):
    D = w.shape[0]
    o_ki = 3 * ATTN_WIDTH + IDX_WIDTH
    o_xr = o_ki + IDX_HEAD_DIM + IDX_HEADS
    pad = IN_COLS_PAD - (o_xr + 2 * LRU_WIDTH)
    return jnp.concatenate([w[:, :o_ki], w[:, o_xr:], w[:, o_ki:o_xr], jnp.zeros((D, pad), w.dtype)],
                           axis=1).astype(BF16)


def _block_diag(w):
    n, c, d = w.shape
    eye = jnp.eye(n, dtype=w.dtype)
    return (w[:, :, None, :] * eye[:, None, :, None]).reshape(n * c, n * d).astype(BF16)


def kernel(x, c, positions, w_mod, b_mod, w_in, conv_w, conv_b, w_gate_a, b_gate_a, w_gate_x, b_gate_x,
           lru_lambda, g_attn_out, g_lru_out, w_out, ln1_g, ln1_b, w_up, w_down, ln2_g, ln2_b):
    B, S, D = x.shape
    L = w_mod.shape[0]
    assert S % KEY_CHUNK == 0 and S % min(ROW_TILE, S) == 0 and D % LANES == 0
    assert w_in.shape[2] == 3 * ATTN_WIDTH + IDX_WIDTH + IDX_HEAD_DIM + IDX_HEADS + 2 * LRU_WIDTH
    alpha = (2.0 * L) ** 0.25

    mod = _mod_call(c, w_mod, b_mod)
    tabs = _rope_call(positions)
    for l in range(L):
        mod3 = mod[l].reshape(B, 1, N_MOD * D)
        qT, k, vT, qiT, kia, kib, wiT, xr, yg = _inproj_call(x, mod3, _prep_w_in(w_in[l]), tabs)
        attn_n = _dsa_call(qT, qiT, wiT, k, vT, kia, kib, g_attn_out[l])
        lru_n = _lru_call(xr, yg, conv_w[l], conv_b[l], _block_diag(w_gate_a[l]), _block_diag(w_gate_x[l]),
                          b_gate_a[l], b_gate_x[l], lru_lambda[l], g_lru_out[l])
        x = _mlp_call(attn_n, lru_n, x, mod3, w_out[l].astype(BF16), ln1_g[l], ln1_b[l],
                      w_up[l].astype(BF16), w_down[l].astype(BF16), ln2_g[l], ln2_b[l], alpha)
    return x
```

```python
import functools
import math

import jax
import jax.numpy as jnp
import numpy as np
from jax import lax
from jax.experimental import pallas as pl
from jax.experimental.pallas import tpu as pltpu

ATTN_HEAD_DIM = 128
ATTN_HEADS = 4
ATTN_WIDTH = ATTN_HEADS * ATTN_HEAD_DIM
IDX_HEADS = 8
IDX_HEAD_DIM = 64
IDX_WIDTH = IDX_HEADS * IDX_HEAD_DIM
TOPK_MAX = 256
LRU_WIDTH = 512
LRU_BLOCKS = 8
LRU_BLOCK = LRU_WIDTH // LRU_BLOCKS
CONV_WIDTH = 4
LRU_C = 8.0
ROPE_THETA = 10000.0
LN_EPS = 1e-5
N_MOD = 6

LANES = 128
SUBLANES = 8
VMEM_LIMIT_BYTES = 56 * 1024 * 1024

KEY_CHUNK = 256
ROW_TILE = 512
IDX_SUB = 128
ATTN_SLAB = 32
SWEEP_GROUP = 4

SEG_Q, SEG_K, SEG_V, SEG_QI, SEG_XR, SEG_YG, SEG_KIW = 0, 512, 1024, 1536, 2048, 2560, 3072
IN_COLS_PAD = 3200

BF16 = jnp.bfloat16
F32 = jnp.float32
INT_MIN = -2 ** 31
KEY_NEG_INF = int(np.int32(np.uint32(0xFF800000) ^ np.uint32(0x7FFFFFFF)))


def _cparams(n_grid):
    return pltpu.CompilerParams(
        dimension_semantics=("arbitrary",) * n_grid,
        vmem_limit_bytes=VMEM_LIMIT_BYTES)


def _mod_body(c_ref, w_ref, b_ref, o_ref):
    c = c_ref[...]
    ca = (c * jax.nn.sigmoid(c)).astype(BF16)
    o_ref[0] = jnp.dot(ca, w_ref[0].astype(BF16), preferred_element_type=F32) + b_ref[0]


def _mod_call(c, w_mod, b_mod):
    L, D, N = w_mod.shape
    B = c.shape[0]
    tn = 2048 if N % 2048 == 0 else N
    return pl.pallas_call(
        _mod_body,
        grid=(L, N // tn),
        in_specs=[pl.BlockSpec((B, D), lambda l, j: (0, 0)),
                  pl.BlockSpec((1, D, tn), lambda l, j: (l, 0, j)),
                  pl.BlockSpec((1, 1, tn), lambda l, j: (l, 0, j))],
        out_specs=pl.BlockSpec((1, B, tn), lambda l, j: (l, 0, j)),
        out_shape=jax.ShapeDtypeStruct((L, B, N), F32),
        compiler_params=_cparams(2),
        name="adaln_mod",
    )(c, w_mod, b_mod.reshape(L, 1, N))


def _rope_body(pos_ref, inv_a_ref, sgn_a_ref, inv_i_ref, sgn_i_ref, ca_ref, sa_ref, ci_ref, si_ref):
    pos = pos_ref[0].astype(F32)
    ang_a = inv_a_ref[...] * pos
    ang_i = inv_i_ref[...] * pos
    ca_ref[0] = jnp.cos(ang_a).T
    sa_ref[0] = (jnp.sin(ang_a) * sgn_a_ref[...]).T
    ci_ref[0] = jnp.cos(ang_i).T
    si_ref[0] = (jnp.sin(ang_i) * sgn_i_ref[...]).T


def _rope_call(positions):
    B, S = positions.shape
    ts = min(ROW_TILE, S)
    half_a, half_i = ATTN_HEAD_DIM // 2, IDX_HEAD_DIM // 2
    inv_a = ROPE_THETA ** (-jnp.arange(0, ATTN_HEAD_DIM, 2, dtype=F32) / ATTN_HEAD_DIM)
    inv_i = ROPE_THETA ** (-jnp.arange(0, IDX_HEAD_DIM, 2, dtype=F32) / IDX_HEAD_DIM)
    inv_a = jnp.tile(inv_a, LANES // half_a).reshape(LANES, 1)
    inv_i = jnp.tile(inv_i, LANES // half_i).reshape(LANES, 1)
    lane = np.arange(LANES)
    sgn_a = jnp.asarray(np.where(lane % ATTN_HEAD_DIM < half_a, -1.0, 1.0).astype(np.float32)).reshape(LANES, 1)
    sgn_i = jnp.asarray(np.where(lane % IDX_HEAD_DIM < half_i, -1.0, 1.0).astype(np.float32)).reshape(LANES, 1)
    col = pl.BlockSpec((LANES, 1), lambda b, i: (0, 0))
    tab = pl.BlockSpec((1, ts, LANES), lambda b, i: (b, i, 0))
    shp = jax.ShapeDtypeStruct((B, S, LANES), F32)
    return pl.pallas_call(
        _rope_body,
        grid=(B, S // ts),
        in_specs=[pl.BlockSpec((1, 1, ts), lambda b, i: (b, 0, i)), col, col, col, col],
        out_specs=[tab, tab, tab, tab],
        out_shape=[shp, shp, shp, shp],
        compiler_params=_cparams(2),
        name="rope_tables",
    )(positions.reshape(B, 1, S), inv_a, sgn_a, inv_i, sgn_i)


def _rope_a(x, cos, sin_signed):
    return x * cos + pltpu.roll(x, ATTN_HEAD_DIM // 2, 1) * sin_signed


def _rope_i(x, cos, sin_signed, lo_half):
    half = IDX_HEAD_DIM // 2
    partner = jnp.where(lo_half, pltpu.roll(x, LANES - half, 1), pltpu.roll(x, half, 1))
    return x * cos + partner * sin_signed


def _inproj_body(x_ref, sh_ref, sc_ref, w_ref, ca_ref, sa_ref, ci_ref, si_ref,
                 qT_ref, k_ref, vT_ref, qiT_ref, kia_ref, kib_ref, wiT_ref, xr_ref, yg_ref):
    tm = x_ref.shape[1]
    n_chunk = tm // KEY_CHUNK
    h = (x_ref[0] * (1.0 + sc_ref[0]) + sh_ref[0]).astype(BF16)
    ca, sa, ci, si = ca_ref[0], sa_ref[0], ci_ref[0], si_ref[0]
    lane = lax.broadcasted_iota(jnp.int32, (tm, LANES), 1)
    lo_half = (lane % IDX_HEAD_DIM) < (IDX_HEAD_DIM // 2)
    q_scale = ATTN_HEAD_DIM ** -0.5 * math.log2(math.e)

    def seg(start, width=LANES):
        return jnp.dot(h, w_ref[:, start:start + width], preferred_element_type=F32)

    def store_T(ref, g, val):
        vt = val.T.astype(BF16)
        for c in range(n_chunk):
            ref[0, c, g * LANES:(g + 1) * LANES, :] = vt[:, c * KEY_CHUNK:(c + 1) * KEY_CHUNK]

    for g in range(ATTN_WIDTH // LANES):
        store_T(qT_ref, g, _rope_a(seg(SEG_Q + g * LANES), ca, sa) * q_scale)
        k_ref[0, :, g * LANES:(g + 1) * LANES] = _rope_a(seg(SEG_K + g * LANES), ca, sa).astype(BF16)
        store_T(vT_ref, g, seg(SEG_V + g * LANES))
        store_T(qiT_ref, g, _rope_i(seg(SEG_QI + g * LANES), ci, si, lo_half))
    xr_ref[0] = seg(SEG_XR, LRU_WIDTH)
    yg_ref[0] = seg(SEG_YG, LRU_WIDTH)
    kiw = seg(SEG_KIW)
    ki = jnp.where(lane < IDX_HEAD_DIM, _rope_i(kiw, ci, si, lo_half), 0.0)
    kia_ref[0] = ki.astype(BF16)
    kib_ref[0] = pltpu.roll(ki, IDX_HEAD_DIM, 1).astype(BF16)
    wiT = kiw.T[IDX_HEAD_DIM:IDX_HEAD_DIM + IDX_HEADS, :] * (IDX_HEADS ** -0.5)
    for c in range(n_chunk):
        wiT_ref[0, c] = wiT[:, c * KEY_CHUNK:(c + 1) * KEY_CHUNK]


def _inproj_call(x, mod3, w_in_p, tabs):
    B, S, D = x.shape
    tm = min(ROW_TILE, S)
    nc = S // KEY_CHUNK
    cpb = tm // KEY_CHUNK
    ca, sa, ci, si = tabs
    tab = pl.BlockSpec((1, tm, LANES), lambda b, i: (b, i, 0))
    tspec = pl.BlockSpec((1, cpb, ATTN_WIDTH, KEY_CHUNK), lambda b, i: (b, i, 0, 0))
    tshape = jax.ShapeDtypeStruct((B, nc, ATTN_WIDTH, KEY_CHUNK), BF16)
    rows = lambda w, dt: (pl.BlockSpec((1, tm, w), lambda b, i: (b, i, 0)), jax.ShapeDtypeStruct((B, S, w), dt))
    k_spec, k_shape = rows(ATTN_WIDTH, BF16)
    ki_spec, ki_shape = rows(LANES, BF16)
    r_spec, r_shape = rows(LRU_WIDTH, F32)
    return pl.pallas_call(
        _inproj_body,
        grid=(B, S // tm),
        in_specs=[pl.BlockSpec((1, tm, D), lambda b, i: (b, i, 0)),
                  pl.BlockSpec((1, 1, D), lambda b, i: (b, 0, 0)),
                  pl.BlockSpec((1, 1, D), lambda b, i: (b, 0, 1)),
                  pl.BlockSpec((D, IN_COLS_PAD), lambda b, i: (0, 0), pipeline_mode=pl.Buffered(1)),
                  tab, tab, tab, tab],
        out_specs=[tspec, k_spec, tspec, tspec, ki_spec, ki_spec,
                   pl.BlockSpec((1, cpb, IDX_HEADS, KEY_CHUNK), lambda b, i: (b, i, 0, 0)),
                   r_spec, r_spec],
        out_shape=[tshape, k_shape, tshape, tshape, ki_shape, ki_shape,
                   jax.ShapeDtypeStruct((B, nc, IDX_HEADS, KEY_CHUNK), F32),
                   r_shape, r_shape],
        compiler_params=_cparams(2),
        name="in_proj",
    )(x, mod3, mod3, w_in_p, ca, sa, ci, si)


def _bit_transpose32(w):
    j, m = 16, 0x0000FFFF
    while j:
        k = 0
        while k < 32:
            t = (w[k] ^ lax.shift_right_logical(w[k + j], jnp.int32(j))) & jnp.int32(m)
            w[k] = w[k] ^ t
            w[k + j] = w[k + j] ^ jnp.left_shift(t, jnp.int32(j))
            k = (k + j + 1) & ~j
        j >>= 1
        m = (m ^ (m << j)) & 0xFFFFFFFF
        if m >= 2 ** 31:
            m -= 2 ** 32


def _fold_rows(x, op):
    r, c = x.shape
    x = x.reshape(r // SUBLANES, SUBLANES, c)
    return op(x, axis=0)


def _dsa_body(qT_ref, qiT_ref, wiT_ref, k_ref, vT_ref, kia_ref, kib_ref, g_ref, o_ref,
              key_s, plane_s, alive_s, bias_s, s_s, p_s, m_s, l_s, acc_s, *, topk, idx_bits, group):
    i = pl.program_id(1)
    tq = KEY_CHUNK
    tk = KEY_CHUNK
    n_chunks = i + 1
    q_pos = i * tq + lax.broadcasted_iota(jnp.int32, (1, tq), 1)

    def score_chunk(c, carry):
        for sub in range(tk // IDX_SUB):
            r0 = pl.multiple_of(c * tk + sub * IDX_SUB, IDX_SUB)
            ka = kia_ref[0, pl.ds(r0, IDX_SUB), :]
            kb = kib_ref[0, pl.ds(r0, IDX_SUB), :]
            acc = jnp.zeros((IDX_SUB, tq), F32)
            for p in range(IDX_HEADS // 2):
                qp = qiT_ref[0, 0, p * LANES:(p + 1) * LANES, :]
                sa = jnp.dot(ka, qp, preferred_element_type=F32)
                sb = jnp.dot(kb, qp, preferred_element_type=F32)
                acc = acc + wiT_ref[0, 0, 2 * p:2 * p + 1, :] * jnp.maximum(sa, 0.0)
                acc = acc + wiT_ref[0, 0, 2 * p + 1:2 * p + 2, :] * jnp.maximum(sb, 0.0)
            acc = acc + 0.0
            bits = pltpu.bitcast(acc, jnp.int32)
            mono = bits ^ ((bits >> 31) & 0x7FFFFFFF)
            k_pos = r0 + lax.broadcasted_iota(jnp.int32, (IDX_SUB, 1), 0)
            key_s[pl.ds(r0, IDX_SUB), :] = jnp.where(k_pos <= q_pos, mono, KEY_NEG_INF)
        c0 = pl.multiple_of(c * tk, tk)
        for lh in range(tq // LANES):
            ls = slice(lh * LANES, (lh + 1) * LANES)
            w = [key_s[pl.ds(c0 + SUBLANES * r, SUBLANES), ls] ^ INT_MIN for r in range(32)]
            _bit_transpose32(w)
            for b in range(32):
                plane_s[c, b, :, ls] = w[b]
        return carry

    lax.fori_loop(0, n_chunks, score_chunk, 0)

    n_groups = (n_chunks + group - 1) // group

    def pad_chunk(c, carry):
        plane_s[c] = jnp.zeros((32, SUBLANES, tq), jnp.int32)
        alive_s[c] = jnp.zeros((SUBLANES, tq), jnp.int32)
        return carry
    lax.fori_loop(n_chunks, n_groups * group, pad_chunk, 0)

    def sweep(j, keep_mask):
        def grp(g, part):
            for cc in range(group):
                c = g * group + cc
                alive = alive_s[c]
                if keep_mask is not None:
                    alive = alive & (plane_s[c, j - 1] ^ keep_mask)
                    alive_s[c] = alive
                part = part + lax.population_count(alive & plane_s[c, j])
            return part
        part = lax.fori_loop(0, n_groups, grp, jnp.zeros((SUBLANES, tq), jnp.int32))
        return part.sum(axis=0, keepdims=True)

    def decide(j, c1, state):
        thr_u, n_gt, n_alive = state
        ok = n_gt + c1 >= topk
        thr_u = thr_u | jnp.where(ok, jnp.left_shift(jnp.int32(1), 31 - j), 0)
        return jnp.where(ok, 0, -1), (thr_u, jnp.where(ok, n_gt, n_gt + c1), jnp.where(ok, c1, n_alive - c1))

    def init_alive(c, carry):
        alive_s[c] = jnp.full((SUBLANES, tq), -1, jnp.int32)
        return carry
    lax.fori_loop(0, n_chunks, init_alive, 0)
    zero = jnp.zeros((1, tq), jnp.int32)
    flip, state = decide(0, sweep(0, None), (zero, zero, zero + n_chunks * tk))

    def bit_step(j, carry):
        flip_prev, state = carry
        return decide(j, sweep(j, jnp.broadcast_to(flip_prev, (SUBLANES, tq))), state)

    _, (thr_u, n_gt, n_alive) = lax.fori_loop(1, 32, bit_step, (flip, state))
    thr = thr_u ^ INT_MIN
    cnt_thr = n_gt + n_alive

    @pl.when(jnp.max(cnt_thr) > topk)
    def _():
        need = topk - n_gt
        def row_idx(c):
            return c * tk + lax.broadcasted_iota(jnp.int32, (tk, 1), 0)
        def count_tied_below(cand):
            def chunk(c, part):
                blk = key_s[pl.ds(pl.multiple_of(c * tk, tk), tk), :]
                hit = jnp.where((blk == thr) & (row_idx(c) < cand), 1, 0)
                return part + _fold_rows(hit, jnp.sum)
            part = lax.fori_loop(0, n_chunks, chunk, jnp.zeros((SUBLANES, tq), jnp.int32))
            return part.sum(axis=0, keepdims=True)
        def idx_step(j, cut):
            cand = cut | jnp.left_shift(jnp.int32(1), idx_bits - 1 - j)
            return jnp.where(count_tied_below(cand) < need, cand, cut)
        cut = lax.fori_loop(0, idx_bits, idx_step, jnp.zeros((1, tq), jnp.int32))
        fix = cnt_thr > topk
        def rewrite(c, carry):
            sl = pl.ds(pl.multiple_of(c * tk, tk), tk)
            blk = key_s[sl, :]
            drop = (blk == thr) & (row_idx(c) > cut) & fix
            key_s[sl, :] = jnp.where(drop, blk - 1, blk)
            return carry
        lax.fori_loop(0, n_chunks, rewrite, 0)

    m_s[...] = jnp.full(m_s.shape, -jnp.inf, F32)
    l_s[...] = jnp.zeros(l_s.shape, F32)
    acc_s[...] = jnp.zeros(acc_s.shape, F32)

    def attn_chunk(c, carry):
        r0 = pl.multiple_of(c * tk, tk)
        k_pos = r0 + lax.broadcasted_iota(jnp.int32, (tk, 1), 0)
        sel = (key_s[pl.ds(r0, tk), :] >= thr) & (k_pos <= q_pos)
        bias_s[...] = jnp.where(sel, 0.0, -jnp.inf)
        for hd in range(ATTN_HEADS):
            hs = slice(hd * ATTN_HEAD_DIM, (hd + 1) * ATTN_HEAD_DIM)
            s_s[hd] = jnp.dot(k_ref[0, pl.ds(r0, tk), hs], qT_ref[0, 0, hs, :],
                              preferred_element_type=F32) + bias_s[...]
        slabs = [slice(r, r + ATTN_SLAB) for r in range(0, tk, ATTN_SLAB)]
        for hd in range(ATTN_HEADS):
            hs = slice(hd * ATTN_HEAD_DIM, (hd + 1) * ATTN_HEAD_DIM)
            m_old = m_s[hd]
            mx = _fold_rows(s_s[hd, slabs[0], :], jnp.max)
            for sl in slabs[1:]:
                mx = jnp.maximum(mx, _fold_rows(s_s[hd, sl, :], jnp.max))
            m_new = jnp.maximum(m_old, mx.max(axis=0, keepdims=True))
            m_safe = jnp.where(m_new == -jnp.inf, 0.0, m_new)
            alpha = jnp.exp2(m_old - m_safe)
            psum = jnp.zeros((SUBLANES, tq), F32)
            for sl in slabs:
                p = jnp.exp2(s_s[hd, sl, :] - m_safe)
                psum = psum + _fold_rows(p, jnp.sum)
                p_s[hd, sl, :] = p.astype(BF16)
            l_s[hd] = alpha * l_s[hd] + psum.sum(axis=0, keepdims=True)
            m_s[hd] = m_new
            acc_s[hd] = alpha * acc_s[hd] + jnp.dot(vT_ref[0, c, hs, :], p_s[hd],
                                                    preferred_element_type=F32)
        return carry

    lax.fori_loop(0, n_chunks, attn_chunk, 0)

    outT = jnp.concatenate([acc_s[hd] / l_s[hd] for hd in range(ATTN_HEADS)], axis=0)
    out = outT.T
    ms = jnp.mean(jnp.square(out), axis=-1, keepdims=True)
    o_ref[0] = (out * lax.rsqrt(ms + LN_EPS) * g_ref[...]).astype(BF16)


def _dsa_call(qT, qiT, wiT, k, vT, kia, kib, g_attn):
    B, S, _ = k.shape
    nc = S // KEY_CHUNK
    topk = min(TOPK_MAX, S // 4)
    idx_bits = max(1, int(math.ceil(math.log2(S))))
    group = math.gcd(SWEEP_GROUP, nc)
    blk_q = pl.BlockSpec((1, 1, ATTN_WIDTH, KEY_CHUNK), lambda b, i: (b, i, 0, 0))
    full = lambda shape: pl.BlockSpec(shape, lambda b, i: (b,) + (0,) * (len(shape) - 1))
    return pl.pallas_call(
        functools.partial(_dsa_body, topk=topk, idx_bits=idx_bits, group=group),
        grid=(B, nc),
        in_specs=[blk_q, blk_q,
                  pl.BlockSpec((1, 1, IDX_HEADS, KEY_CHUNK), lambda b, i: (b, i, 0, 0)),
                  full((1, S, ATTN_WIDTH)),
                  full((1, nc, ATTN_WIDTH, KEY_CHUNK)),
                  full((1, S, LANES)), full((1, S, LANES)),
                  pl.BlockSpec((1, ATTN_WIDTH), lambda b, i: (0, 0))],
        out_specs=pl.BlockSpec((1, KEY_CHUNK, ATTN_WIDTH), lambda b, i: (b, i, 0)),
        out_shape=jax.ShapeDtypeStruct((B, S, ATTN_WIDTH), BF16),
        scratch_shapes=[pltpu.VMEM((S, KEY_CHUNK), jnp.int32),
                        pltpu.VMEM((nc, 32, SUBLANES, KEY_CHUNK), jnp.int32),
                        pltpu.VMEM((nc, SUBLANES, KEY_CHUNK), jnp.int32),
                        pltpu.VMEM((KEY_CHUNK, KEY_CHUNK), F32),
                        pltpu.VMEM((ATTN_HEADS, KEY_CHUNK, KEY_CHUNK), F32),
                        pltpu.VMEM((ATTN_HEADS, KEY_CHUNK, KEY_CHUNK), BF16),
                        pltpu.VMEM((ATTN_HEADS, 1, KEY_CHUNK), F32),
                        pltpu.VMEM((ATTN_HEADS, 1, KEY_CHUNK), F32),
                        pltpu.VMEM((ATTN_HEADS, ATTN_HEAD_DIM, KEY_CHUNK), F32)],
        compiler_params=_cparams(2),
        name="dsa_attention",
    )(qT, qiT, wiT, k, vT, kia, kib, g_attn.reshape(1, ATTN_WIDTH))


def _lru_body(xr_ref, yg_ref, cw_ref, cb_ref, wa_ref, wx_ref, ba_ref, bx_ref, lam_ref, g_ref, o_ref,
              xbuf_s, a_s, u_s, h_s, carry_s):
    ts = xr_ref.shape[1]
    W = xr_ref.shape[2]
    pad = SUBLANES

    @pl.when(pl.program_id(1) == 0)
    def _():
        xbuf_s[0:pad, :] = jnp.zeros((pad, W), F32)
        carry_s[...] = jnp.zeros(carry_s.shape, F32)

    xbuf_s[pad:pad + ts, :] = xr_ref[0]
    xc = cb_ref[...]
    for j in range(CONV_WIDTH):
        xc = xc + xbuf_s[pad - (CONV_WIDTH - 1) + j:pad - (CONV_WIDTH - 1) + j + ts, :] * cw_ref[j:j + 1, :]
    xbuf_s[0:pad, :] = xbuf_s[ts:ts + pad, :]

    xcb = xc.astype(BF16)
    r = jax.nn.sigmoid(jnp.dot(xcb, wa_ref[...], preferred_element_type=F32) + ba_ref[...])
    gi = jax.nn.sigmoid(jnp.dot(xcb, wx_ref[...], preferred_element_type=F32) + bx_ref[...])
    neg_lam = -lam_ref[...]
    softplus = jnp.maximum(neg_lam, 0.0) + jnp.log1p(jnp.exp(-jnp.abs(neg_lam)))
    log_a = -LRU_C * r * softplus
    a = jnp.exp(log_a)
    u = jnp.sqrt(-jnp.tanh(log_a) * (a * a + 1.0)) * (gi * xc)

    row = lax.broadcasted_iota(jnp.int32, (ts, 1), 0) % SUBLANES
    d = 1
    while d < SUBLANES:
        keep = row >= d
        a_sh = jnp.where(keep, pltpu.roll(a, d, 0), 1.0)
        u_sh = jnp.where(keep, pltpu.roll(u, d, 0), 0.0)
        u = a * u_sh + u
        a = a * a_sh
        d *= 2
    a_s[...] = a
    u_s[...] = u

    def group(g, hprev):
        sl = pl.ds(pl.multiple_of(g * SUBLANES, SUBLANES), SUBLANES)
        hg = a_s[sl, :] * hprev + u_s[sl, :]
        h_s[sl, :] = hg
        return hg[SUBLANES - 1:SUBLANES, :]

    carry_s[...] = lax.fori_loop(0, ts // SUBLANES, group, carry_s[...], unroll=8)

    y = h_s[...] * jax.nn.gelu(yg_ref[0], approximate=True)
    ms = jnp.mean(jnp.square(y), axis=-1, keepdims=True)
    o_ref[0] = (y * lax.rsqrt(ms + LN_EPS) * g_ref[...]).astype(BF16)


def _lru_call(xr, yg, conv_w, conv_b, wa_bd, wx_bd, b_a, b_x, lam, g_lru):
    B, S, W = xr.shape
    ts = min(ROW_TILE, S)
    rows = pl.BlockSpec((1, ts, W), lambda b, i: (b, i, 0))
    vec = pl.BlockSpec((1, W), lambda b, i: (0, 0))
    mat = pl.BlockSpec((W, W), lambda b, i: (0, 0))
    return pl.pallas_call(
        _lru_body,
        grid=(B, S // ts),
        in_specs=[rows, rows, pl.BlockSpec((CONV_WIDTH, W), lambda b, i: (0, 0)), vec, mat, mat,
                  vec, vec, vec, vec],
        out_specs=rows,
        out_shape=jax.ShapeDtypeStruct((B, S, W), BF16),
        scratch_shapes=[pltpu.VMEM((ts + SUBLANES, W), F32),
                        pltpu.VMEM((ts, W), F32), pltpu.VMEM((ts, W), F32), pltpu.VMEM((ts, W), F32),
                        pltpu.VMEM((1, W), F32)],
        compiler_params=_cparams(2),
        name="rg_lru",
    )(xr, yg, conv_w, conv_b.reshape(1, W), wa_bd, wx_bd, b_a.reshape(1, W), b_x.reshape(1, W),
      lam.reshape(1, W), g_lru.reshape(1, W))


def _layer_norm(y, g, b):
    mu = jnp.mean(y, axis=-1, keepdims=True)
    yc = y - mu
    var = jnp.mean(jnp.square(yc), axis=-1, keepdims=True)
    return yc * lax.rsqrt(var + LN_EPS) * g + b


def _mlp_body(at_ref, lr_ref, x_ref, g1_ref, sh2_ref, sc2_ref, g2_ref, wo_ref, l1g_ref, l1b_ref,
              wu_ref, wd_ref, l2g_ref, l2b_ref, o_ref, *, alpha, ff_chunk):
    aw = at_ref.shape[2]
    mix = jnp.dot(at_ref[0], wo_ref[0:aw, :], preferred_element_type=F32)
    mix = mix + jnp.dot(lr_ref[0], wo_ref[aw:, :], preferred_element_type=F32)
    x1 = _layer_norm(alpha * x_ref[0] + (1.0 + g1_ref[0]) * mix, l1g_ref[...], l1b_ref[...])
    h2 = (x1 * (1.0 + sc2_ref[0]) + sh2_ref[0]).astype(BF16)
    d_ff = wu_ref.shape[1]
    ff = jnp.zeros(x1.shape, F32)
    for c in range(d_ff // ff_chunk):
        cs = slice(c * ff_chunk, (c + 1) * ff_chunk)
        up = jnp.dot(h2, wu_ref[:, cs], preferred_element_type=F32)
        act = jnp.square(jnp.maximum(up, 0.0)).astype(BF16)
        ff = ff + jnp.dot(act, wd_ref[cs, :], preferred_element_type=F32)
    o_ref[0] = _layer_norm(alpha * x1 + (1.0 + g2_ref[0]) * ff, l2g_ref[...], l2b_ref[...])


def _mlp_call(attn_n, lru_n, x, mod3, w_out, ln1_g, ln1_b, w_up, w_down, ln2_g, ln2_b, alpha):
    B, S, D = x.shape
    tm = min(ROW_TILE, S)
    d_ff = w_up.shape[1]
    half = pl.BlockSpec((1, tm, attn_n.shape[2]), lambda b, i: (b, i, 0))
    rows = pl.BlockSpec((1, tm, D), lambda b, i: (b, i, 0))
    modv = lambda j: pl.BlockSpec((1, 1, D), lambda b, i: (b, 0, j))
    vec = pl.BlockSpec((1, D), lambda b, i: (0, 0))
    wspec = lambda shape: pl.BlockSpec(shape, lambda b, i: (0, 0), pipeline_mode=pl.Buffered(1))
    return pl.pallas_call(
        functools.partial(_mlp_body, alpha=alpha, ff_chunk=min(1024, d_ff)),
        grid=(B, S // tm),
        in_specs=[half, half, rows, modv(2), modv(3), modv(4), modv(5),
                  wspec((D, D)), vec, vec, wspec((D, d_ff)), wspec((d_ff, D)), vec, vec],
        out_specs=rows,
        out_shape=jax.ShapeDtypeStruct((B, S, D), F32),
        compiler_params=_cparams(2),
        name="out_mlp",
    )(attn_n, lru_n, x, mod3, mod3, mod3, mod3, w_out, ln1_g.reshape(1, D), ln1_b.reshape(1, D),
      w_up, w_down, ln2_g.reshape(1, D), ln2_b.reshape(1, D))


def _prep_w_in(w):
    D = w.shape[0]
    o_ki = 3 * ATTN_WIDTH + IDX_WIDTH
    o_xr = o_ki + IDX_HEAD_DIM + IDX_HEADS
    pad = IN_COLS_PAD - (o_xr + 2 * LRU_WIDTH)
    return jnp.concatenate([w[:, :o_ki], w[:, o_xr:], w[:, o_ki:o_xr], jnp.zeros((D, pad), w.dtype)],
                           axis=1).astype(BF16)


def _block_diag(w):
    n, c, d = w.shape
    eye = jnp.eye(n, dtype=w.dtype)
    return (w[:, :, None, :] * eye[:, None, :, None]).reshape(n * c, n * d).astype(BF16)


def kernel(x, c, positions, w_mod, b_mod, w_in, conv_w, conv_b, w_gate_a, b_gate_a, w_gate_x, b_gate_x,
           lru_lambda, g_attn_out, g_lru_out, w_out, ln1_g, ln1_b, w_up, w_down, ln2_g, ln2_b):
    B, S, D = x.shape
    L = w_mod.shape[0]
    assert S % KEY_CHUNK == 0 and S % min(ROW_TILE, S) == 0 and D % LANES == 0
    assert w_in.shape[2] == 3 * ATTN_WIDTH + IDX_WIDTH + IDX_HEAD_DIM + IDX_HEADS + 2 * LRU_WIDTH
    alpha = (2.0 * L) ** 0.25

    mod = _mod_call(c, w_mod, b_mod)
    tabs = _rope_call(positions)
    for l in range(L):
        mod3 = mod[l].reshape(B, 1, N_MOD * D)
        qT, k, vT, qiT, kia, kib, wiT, xr, yg = _inproj_call(x, mod3, _prep_w_in(w_in[l]), tabs)
        attn_n = _dsa_call(qT, qiT, wiT, k, vT, kia, kib, g_attn_out[l])
        lru_n = _lru_call(xr, yg, conv_w[l], conv_b[l], _block_diag(w_gate_a[l]), _block_diag(w_gate_x[l]),
                          b_gate_a[l], b_gate_x[l], lru_lambda[l], g_lru_out[l])
        x = _mlp_call(attn_n, lru_n, x, mod3, w_out[l].astype(BF16), ln1_g[l], ln1_b[l],
                      w_up[l].astype(BF16), w_down[l].astype(BF16), ln2_g[l], ln2_b[l], alpha)
    return x
```

```python
import functools
import math

import jax
import jax.numpy as jnp
import numpy as np
from jax import lax
from jax.experimental import pallas as pl
from jax.experimental.pallas import tpu as pltpu

ATTN_HEAD_DIM = 128
ATTN_HEADS = 4
ATTN_WIDTH = ATTN_HEADS * ATTN_HEAD_DIM
IDX_HEADS = 8
IDX_HEAD_DIM = 64
IDX_WIDTH = IDX_HEADS * IDX_HEAD_DIM
TOPK_MAX = 256
LRU_WIDTH = 512
LRU_BLOCKS = 8
LRU_BLOCK = LRU_WIDTH // LRU_BLOCKS
CONV_WIDTH = 4
LRU_C = 8.0
ROPE_THETA = 10000.0
LN_EPS = 1e-5
N_MOD = 6

LANES = 128
SUBLANES = 8
MXU_COLS = 256
VMEM_LIMIT_BYTES = 56 * 1024 * 1024

KEY_CHUNK = 256
ROW_TILE = 512
IDX_SUB = 128
ATTN_SLAB = 32
SWEEP_GROUP = 4

SEG_Q, SEG_K, SEG_V, SEG_QI, SEG_XR, SEG_YG, SEG_KIW = 0, 512, 1024, 1536, 2048, 2560, 3072
IN_COLS_PAD = 3200

BF16 = jnp.bfloat16
F32 = jnp.float32
INT_MIN = -2 ** 31
KEY_NEG_INF = int(np.int32(np.uint32(0xFF800000) ^ np.uint32(0x7FFFFFFF)))


def _cparams(n_grid):
    return pltpu.CompilerParams(
        dimension_semantics=("arbitrary",) * n_grid,
        vmem_limit_bytes=VMEM_LIMIT_BYTES)


def _mod_body(c_ref, w_ref, b_ref, o_ref):
    c = c_ref[...]
    ca = (c * jax.nn.sigmoid(c)).astype(BF16)
    o_ref[0] = jnp.dot(ca, w_ref[0].astype(BF16), preferred_element_type=F32) + b_ref[0]


def _mod_call(c, w_mod, b_mod):
    L, D, N = w_mod.shape
    B = c.shape[0]
    tn = 2048 if N % 2048 == 0 else N
    return pl.pallas_call(
        _mod_body,
        grid=(L, N // tn),
        in_specs=[pl.BlockSpec((B, D), lambda l, j: (0, 0)),
                  pl.BlockSpec((1, D, tn), lambda l, j: (l, 0, j)),
                  pl.BlockSpec((1, 1, tn), lambda l, j: (l, 0, j))],
        out_specs=pl.BlockSpec((1, B, tn), lambda l, j: (l, 0, j)),
        out_shape=jax.ShapeDtypeStruct((L, B, N), F32),
        compiler_params=_cparams(2),
        name="adaln_mod",
    )(c, w_mod, b_mod.reshape(L, 1, N))


def _rope_body(pos_ref, inv_a_ref, sgn_a_ref, inv_i_ref, sgn_i_ref, ca_ref, sa_ref, ci_ref, si_ref):
    pos = pos_ref[0].astype(F32)
    ang_a = inv_a_ref[...] * pos
    ang_i = inv_i_ref[...] * pos
    ca_ref[0] = jnp.cos(ang_a).T
    sa_ref[0] = (jnp.sin(ang_a) * sgn_a_ref[...]).T
    ci_ref[0] = jnp.cos(ang_i).T
    si_ref[0] = (jnp.sin(ang_i) * sgn_i_ref[...]).T


def _rope_call(positions):
    B, S = positions.shape
    ts = min(ROW_TILE, S)
    half_a, half_i = ATTN_HEAD_DIM // 2, IDX_HEAD_DIM // 2
    inv_a = ROPE_THETA ** (-jnp.arange(0, ATTN_HEAD_DIM, 2, dtype=F32) / ATTN_HEAD_DIM)
    inv_i = ROPE_THETA ** (-jnp.arange(0, IDX_HEAD_DIM, 2, dtype=F32) / IDX_HEAD_DIM)
    inv_a = jnp.tile(inv_a, LANES // half_a).reshape(LANES, 1)
    inv_i = jnp.tile(inv_i, LANES // half_i).reshape(LANES, 1)
    lane = np.arange(LANES)
    sgn_a = jnp.asarray(np.where(lane % ATTN_HEAD_DIM < half_a, -1.0, 1.0).astype(np.float32)).reshape(LANES, 1)
    sgn_i = jnp.asarray(np.where(lane % IDX_HEAD_DIM < half_i, -1.0, 1.0).astype(np.float32)).reshape(LANES, 1)
    col = pl.BlockSpec((LANES, 1), lambda b, i: (0, 0))
    tab = pl.BlockSpec((1, ts, LANES), lambda b, i: (b, i, 0))
    shp = jax.ShapeDtypeStruct((B, S, LANES), F32)
    return pl.pallas_call(
        _rope_body,
        grid=(B, S // ts),
        in_specs=[pl.BlockSpec((1, 1, ts), lambda b, i: (b, 0, i)), col, col, col, col],
        out_specs=[tab, tab, tab, tab],
        out_shape=[shp, shp, shp, shp],
        compiler_params=_cparams(2),
        name="rope_tables",
    )(positions.reshape(B, 1, S), inv_a, sgn_a, inv_i, sgn_i)


def _rope_a(x, cos, sin_signed):
    return x * cos + pltpu.roll(x, ATTN_HEAD_DIM // 2, 1) * sin_signed


def _rope_i(x, cos, sin_signed, lo_half):
    half = IDX_HEAD_DIM // 2
    partner = jnp.where(lo_half, pltpu.roll(x, LANES - half, 1), pltpu.roll(x, half, 1))
    return x * cos + partner * sin_signed


def _inproj_body(x_ref, sh_ref, sc_ref, w_ref, ca_ref, sa_ref, ci_ref, si_ref,
                 qT_ref, k_ref, vT_ref, qiT_ref, kia_ref, kib_ref, wiT_ref, xr_ref, yg_ref):
    tm = x_ref.shape[1]
    n_chunk = tm // KEY_CHUNK
    h = (x_ref[0] * (1.0 + sc_ref[0]) + sh_ref[0]).astype(BF16)
    ca, sa, ci, si = ca_ref[0], sa_ref[0], ci_ref[0], si_ref[0]
    lane = lax.broadcasted_iota(jnp.int32, (tm, LANES), 1)
    lo_half = (lane % IDX_HEAD_DIM) < (IDX_HEAD_DIM // 2)
    q_scale = ATTN_HEAD_DIM ** -0.5 * math.log2(math.e)

    def seg(start, width=LANES):
        return jnp.dot(h, w_ref[:, start:start + width], preferred_element_type=F32)

    def store_T(ref, g, val):
        vt = val.T.astype(BF16)
        for c in range(n_chunk):
            ref[0, c, g * LANES:(g + 1) * LANES, :] = vt[:, c * KEY_CHUNK:(c + 1) * KEY_CHUNK]

    def lane_groups(start):
        for pair in range(ATTN_WIDTH // MXU_COLS):
            wide = seg(start + pair * MXU_COLS, MXU_COLS)
            for half in range(MXU_COLS // LANES):
                yield pair * (MXU_COLS // LANES) + half, wide[:, half * LANES:(half + 1) * LANES]

    for g, val in lane_groups(SEG_Q):
        store_T(qT_ref, g, _rope_a(val, ca, sa) * q_scale)
    for g, val in lane_groups(SEG_K):
        k_ref[0, :, g * LANES:(g + 1) * LANES] = _rope_a(val, ca, sa).astype(BF16)
    for g, val in lane_groups(SEG_V):
        store_T(vT_ref, g, val)
    for g, val in lane_groups(SEG_QI):
        store_T(qiT_ref, g, _rope_i(val, ci, si, lo_half))
    xr_ref[0] = seg(SEG_XR, LRU_WIDTH)
    yg_ref[0] = seg(SEG_YG, LRU_WIDTH)
    kiw = seg(SEG_KIW)
    ki = jnp.where(lane < IDX_HEAD_DIM, _rope_i(kiw, ci, si, lo_half), 0.0)
    kia_ref[0] = ki.astype(BF16)
    kib_ref[0] = pltpu.roll(ki, IDX_HEAD_DIM, 1).astype(BF16)
    wiT = kiw.T[IDX_HEAD_DIM:IDX_HEAD_DIM + IDX_HEADS, :] * (IDX_HEADS ** -0.5)
    for c in range(n_chunk):
        wiT_ref[0, c] = wiT[:, c * KEY_CHUNK:(c + 1) * KEY_CHUNK]


def _inproj_call(x, mod3, w_in_p, tabs):
    B, S, D = x.shape
    tm = min(ROW_TILE, S)
    nc = S // KEY_CHUNK
    cpb = tm // KEY_CHUNK
    ca, sa, ci, si = tabs
    tab = pl.BlockSpec((1, tm, LANES), lambda b, i: (b, i, 0))
    tspec = pl.BlockSpec((1, cpb, ATTN_WIDTH, KEY_CHUNK), lambda b, i: (b, i, 0, 0))
    tshape = jax.ShapeDtypeStruct((B, nc, ATTN_WIDTH, KEY_CHUNK), BF16)
    rows = lambda w, dt: (pl.BlockSpec((1, tm, w), lambda b, i: (b, i, 0)), jax.ShapeDtypeStruct((B, S, w), dt))
    k_spec, k_shape = rows(ATTN_WIDTH, BF16)
    ki_spec, ki_shape = rows(LANES, BF16)
    r_spec, r_shape = rows(LRU_WIDTH, F32)
    return pl.pallas_call(
        _inproj_body,
        grid=(B, S // tm),
        in_specs=[pl.BlockSpec((1, tm, D), lambda b, i: (b, i, 0)),
                  pl.BlockSpec((1, 1, D), lambda b, i: (b, 0, 0)),
                  pl.BlockSpec((1, 1, D), lambda b, i: (b, 0, 1)),
                  pl.BlockSpec((D, IN_COLS_PAD), lambda b, i: (0, 0), pipeline_mode=pl.Buffered(1)),
                  tab, tab, tab, tab],
        out_specs=[tspec, k_spec, tspec, tspec, ki_spec, ki_spec,
                   pl.BlockSpec((1, cpb, IDX_HEADS, KEY_CHUNK), lambda b, i: (b, i, 0, 0)),
                   r_spec, r_spec],
        out_shape=[tshape, k_shape, tshape, tshape, ki_shape, ki_shape,
                   jax.ShapeDtypeStruct((B, nc, IDX_HEADS, KEY_CHUNK), F32),
                   r_shape, r_shape],
        compiler_params=_cparams(2),
        name="in_proj",
    )(x, mod3, mod3, w_in_p, ca, sa, ci, si)


def _bit_transpose32(w):
    j, m = 16, 0x0000FFFF
    while j:
        k = 0
        while k < 32:
            t = (w[k] ^ lax.shift_right_logical(w[k + j], jnp.int32(j))) & jnp.int32(m)
            w[k] = w[k] ^ t
            w[k + j] = w[k + j] ^ jnp.left_shift(t, jnp.int32(j))
            k = (k + j + 1) & ~j
        j >>= 1
        m = (m ^ (m << j)) & 0xFFFFFFFF
        if m >= 2 ** 31:
            m -= 2 ** 32


def _fold_rows(x, op):
    r, c = x.shape
    x = x.reshape(r // SUBLANES, SUBLANES, c)
    return op(x, axis=0)


def _dsa_body(qT_ref, qiT_ref, wiT_ref, k_ref, vT_ref, kia_ref, kib_ref, g_ref, o_ref,
              key_s, plane_s, alive_s, bias_s, s_s, p_s, m_s, l_s, acc_s, *, topk, idx_bits, group):
    i = pl.program_id(1)
    tq = KEY_CHUNK
    tk = KEY_CHUNK
    n_chunks = i + 1
    q_pos = i * tq + lax.broadcasted_iota(jnp.int32, (1, tq), 1)

    def score_chunk(c, diagonal):
        for sub in range(tk // IDX_SUB):
            r0 = pl.multiple_of(c * tk + sub * IDX_SUB, IDX_SUB)
            ka = kia_ref[0, pl.ds(r0, IDX_SUB), :]
            kb = kib_ref[0, pl.ds(r0, IDX_SUB), :]
            acc = jnp.zeros((IDX_SUB, tq), F32)
            for p in range(IDX_HEADS // 2):
                qp = qiT_ref[0, 0, p * LANES:(p + 1) * LANES, :]
                sa = jnp.dot(ka, qp, preferred_element_type=F32)
                sb = jnp.dot(kb, qp, preferred_element_type=F32)
                acc = acc + wiT_ref[0, 0, 2 * p:2 * p + 1, :] * jnp.maximum(sa, 0.0)
                acc = acc + wiT_ref[0, 0, 2 * p + 1:2 * p + 2, :] * jnp.maximum(sb, 0.0)
            acc = acc + 0.0
            bits = pltpu.bitcast(acc, jnp.int32)
            mono = bits ^ ((bits >> 31) & 0x7FFFFFFF)
            if diagonal:
                k_pos = r0 + lax.broadcasted_iota(jnp.int32, (IDX_SUB, 1), 0)
                mono = jnp.where(k_pos <= q_pos, mono, KEY_NEG_INF)
            key_s[pl.ds(r0, IDX_SUB), :] = mono
        c0 = pl.multiple_of(c * tk, tk)
        for lh in range(tq // LANES):
            ls = slice(lh * LANES, (lh + 1) * LANES)
            w = [key_s[pl.ds(c0 + SUBLANES * r, SUBLANES), ls] ^ INT_MIN for r in range(32)]
            _bit_transpose32(w)
            for b in range(32):
                plane_s[c, b, :, ls] = w[b]

    def below_diagonal(body):
        def run(c, carry):
            body(c, False)
            return carry
        lax.fori_loop(0, i, run, 0)

    below_diagonal(score_chunk)
    score_chunk(i, True)

    n_groups = (n_chunks + group - 1) // group

    def pad_chunk(c, carry):
        plane_s[c] = jnp.zeros((32, SUBLANES, tq), jnp.int32)
        alive_s[c] = jnp.zeros((SUBLANES, tq), jnp.int32)
        return carry
    lax.fori_loop(n_chunks, n_groups * group, pad_chunk, 0)

    def sweep(j, keep_mask):
        def grp(g, part):
            for cc in range(group):
                c = g * group + cc
                alive = alive_s[c]
                if keep_mask is not None:
                    alive = alive & (plane_s[c, j - 1] ^ keep_mask)
                    alive_s[c] = alive
                part = part + lax.population_count(alive & plane_s[c, j])
            return part
        part = lax.fori_loop(0, n_groups, grp, jnp.zeros((SUBLANES, tq), jnp.int32))
        return part.sum(axis=0, keepdims=True)

    def decide(j, c1, state):
        thr_u, n_gt, n_alive = state
        ok = n_gt + c1 >= topk
        thr_u = thr_u | jnp.where(ok, jnp.left_shift(jnp.int32(1), 31 - j), 0)
        return jnp.where(ok, 0, -1), (thr_u, jnp.where(ok, n_gt, n_gt + c1), jnp.where(ok, c1, n_alive - c1))

    def init_alive(c, carry):
        alive_s[c] = jnp.full((SUBLANES, tq), -1, jnp.int32)
        return carry
    lax.fori_loop(0, n_chunks, init_alive, 0)
    zero = jnp.zeros((1, tq), jnp.int32)
    flip, state = decide(0, sweep(0, None), (zero, zero, zero + n_chunks * tk))

    def bit_step(j, carry):
        flip_prev, state = carry
        return decide(j, sweep(j, jnp.broadcast_to(flip_prev, (SUBLANES, tq))), state)

    flip_last, (thr_u, n_gt, n_alive) = lax.fori_loop(1, 32, bit_step, (flip, state))
    thr = thr_u ^ INT_MIN
    cnt_thr = n_gt + n_alive

    @pl.when(jnp.max(cnt_thr) > topk)
    def _():
        need = topk - n_gt
        def row_idx(c):
            return c * tk + lax.broadcasted_iota(jnp.int32, (tk, 1), 0)
        flip8 = jnp.broadcast_to(flip_last, (SUBLANES, tq))
        def settle(c, carry):
            alive_s[c] = alive_s[c] & (plane_s[c, 31] ^ flip8)
            return carry
        lax.fori_loop(0, n_chunks, settle, 0)
        def count_tied_below(cand):
            def grp(g, part):
                for cc in range(group):
                    c = g * group + cc
                    first_row = c * tk + lax.broadcasted_iota(jnp.int32, (SUBLANES, 1), 0)
                    n_below = jnp.clip((cand - first_row + (SUBLANES - 1)) >> 3, 0, 32)
                    top_bits = jnp.where(n_below > 0,
                                         jnp.left_shift(jnp.int32(-1), 32 - jnp.maximum(n_below, 1)), 0)
                    part = part + lax.population_count(alive_s[c] & top_bits)
                return part
            part = lax.fori_loop(0, n_groups, grp, jnp.zeros((SUBLANES, tq), jnp.int32))
            return part.sum(axis=0, keepdims=True)
        def idx_step(j, cut):
            cand = cut | jnp.left_shift(jnp.int32(1), idx_bits - 1 - j)
            return jnp.where(count_tied_below(cand) < need, cand, cut)
        cut = lax.fori_loop(0, idx_bits, idx_step, jnp.zeros((1, tq), jnp.int32))
        fix = cnt_thr > topk
        def rewrite(c, carry):
            sl = pl.ds(pl.multiple_of(c * tk, tk), tk)
            blk = key_s[sl, :]
            drop = (blk == thr) & (row_idx(c) > cut) & fix
            key_s[sl, :] = jnp.where(drop, blk - 1, blk)
            return carry
        lax.fori_loop(0, n_chunks, rewrite, 0)

    m_s[...] = jnp.full(m_s.shape, -jnp.inf, F32)
    l_s[...] = jnp.zeros(l_s.shape, F32)
    acc_s[...] = jnp.zeros(acc_s.shape, F32)

    heads = [slice(hd * ATTN_HEAD_DIM, (hd + 1) * ATTN_HEAD_DIM) for hd in range(ATTN_HEADS)]
    slabs = [slice(r, r + ATTN_SLAB) for r in range(0, tk, ATTN_SLAB)]

    def attn_chunk(c, diagonal):
        r0 = pl.multiple_of(c * tk, tk)
        for sl in slabs:
            sel = key_s[pl.ds(r0 + sl.start, ATTN_SLAB), :] >= thr
            if diagonal:
                k_pos = r0 + sl.start + lax.broadcasted_iota(jnp.int32, (ATTN_SLAB, 1), 0)
                sel = sel & (k_pos <= q_pos)
            bias_s[sl, :] = jnp.where(sel, 0.0, -jnp.inf)
        for hd, hs in enumerate(heads):
            s_s[hd] = jnp.dot(k_ref[0, pl.ds(r0, tk), hs], qT_ref[0, 0, hs, :],
                              preferred_element_type=F32) + bias_s[...]
        m_safe, alpha = [], []
        for hd in range(ATTN_HEADS):
            m_old = m_s[hd]
            mx = _fold_rows(s_s[hd, slabs[0], :], jnp.max)
            for sl in slabs[1:]:
                mx = jnp.maximum(mx, _fold_rows(s_s[hd, sl, :], jnp.max))
            m_new = jnp.maximum(m_old, mx.max(axis=0, keepdims=True))
            m_s[hd] = m_new
            m_safe.append(jnp.where(m_new == -jnp.inf, 0.0, m_new))
            alpha.append(jnp.exp2(m_old - m_safe[hd]))
        for hd in range(ATTN_HEADS):
            psum = jnp.zeros((SUBLANES, tq), F32)
            for sl in slabs:
                p = jnp.exp2(s_s[hd, sl, :] - m_safe[hd])
                psum = psum + _fold_rows(p, jnp.sum)
                p_s[hd, sl, :] = p.astype(BF16)
            l_s[hd] = alpha[hd] * l_s[hd] + psum.sum(axis=0, keepdims=True)
        for hd, hs in enumerate(heads):
            acc_s[hd] = alpha[hd] * acc_s[hd] + jnp.dot(vT_ref[0, c, hs, :], p_s[hd],
                                                        preferred_element_type=F32)

    below_diagonal(attn_chunk)
    attn_chunk(i, True)

    outT = jnp.concatenate([acc_s[hd] / l_s[hd] for hd in range(ATTN_HEADS)], axis=0)
    out = outT.T
    ms = jnp.mean(jnp.square(out), axis=-1, keepdims=True)
    o_ref[0] = (out * lax.rsqrt(ms + LN_EPS) * g_ref[...]).astype(BF16)


def _dsa_call(qT, qiT, wiT, k, vT, kia, kib, g_attn):
    B, S, _ = k.shape
    nc = S // KEY_CHUNK
    topk = min(TOPK_MAX, S // 4)
    idx_bits = max(1, int(math.ceil(math.log2(S))))
    group = math.gcd(SWEEP_GROUP, nc)
    blk_q = pl.BlockSpec((1, 1, ATTN_WIDTH, KEY_CHUNK), lambda b, i: (b, i, 0, 0))
    full = lambda shape: pl.BlockSpec(shape, lambda b, i: (b,) + (0,) * (len(shape) - 1))
    return pl.pallas_call(
        functools.partial(_dsa_body, topk=topk, idx_bits=idx_bits, group=group),
        grid=(B, nc),
        in_specs=[blk_q, blk_q,
                  pl.BlockSpec((1, 1, IDX_HEADS, KEY_CHUNK), lambda b, i: (b, i, 0, 0)),
                  full((1, S, ATTN_WIDTH)),
                  full((1, nc, ATTN_WIDTH, KEY_CHUNK)),
                  full((1, S, LANES)), full((1, S, LANES)),
                  pl.BlockSpec((1, ATTN_WIDTH), lambda b, i: (0, 0))],
        out_specs=pl.BlockSpec((1, KEY_CHUNK, ATTN_WIDTH), lambda b, i: (b, i, 0)),
        out_shape=jax.ShapeDtypeStruct((B, S, ATTN_WIDTH), BF16),
        scratch_shapes=[pltpu.VMEM((S, KEY_CHUNK), jnp.int32),
                        pltpu.VMEM((nc, 32, SUBLANES, KEY_CHUNK), jnp.int32),
                        pltpu.VMEM((nc, SUBLANES, KEY_CHUNK), jnp.int32),
                        pltpu.VMEM((KEY_CHUNK, KEY_CHUNK), F32),
                        pltpu.VMEM((ATTN_HEADS, KEY_CHUNK, KEY_CHUNK), F32),
                        pltpu.VMEM((ATTN_HEADS, KEY_CHUNK, KEY_CHUNK), BF16),
                        pltpu.VMEM((ATTN_HEADS, 1, KEY_CHUNK), F32),
                        pltpu.VMEM((ATTN_HEADS, 1, KEY_CHUNK), F32),
                        pltpu.VMEM((ATTN_HEADS, ATTN_HEAD_DIM, KEY_CHUNK), F32)],
        compiler_params=_cparams(2),
        name="dsa_attention",
    )(qT, qiT, wiT, k, vT, kia, kib, g_attn.reshape(1, ATTN_WIDTH))


def _lru_body(xr_ref, yg_ref, cw_ref, cb_ref, wa_ref, wx_ref, ba_ref, bx_ref, lam_ref, g_ref, o_ref,
              xbuf_s, a_s, u_s, h_s, carry_s):
    ts = xr_ref.shape[1]
    W = xr_ref.shape[2]
    pad = SUBLANES

    @pl.when(pl.program_id(1) == 0)
    def _():
        xbuf_s[0:pad, :] = jnp.zeros((pad, W), F32)
        carry_s[...] = jnp.zeros(carry_s.shape, F32)

    xbuf_s[pad:pad + ts, :] = xr_ref[0]
    xc = cb_ref[...]
    for j in range(CONV_WIDTH):
        xc = xc + xbuf_s[pad - (CONV_WIDTH - 1) + j:pad - (CONV_WIDTH - 1) + j + ts, :] * cw_ref[j:j + 1, :]
    xbuf_s[0:pad, :] = xbuf_s[ts:ts + pad, :]

    xcb = xc.astype(BF16)
    r = jax.nn.sigmoid(jnp.dot(xcb, wa_ref[...], preferred_element_type=F32) + ba_ref[...])
    gi = jax.nn.sigmoid(jnp.dot(xcb, wx_ref[...], preferred_element_type=F32) + bx_ref[...])
    neg_lam = -lam_ref[...]
    softplus = jnp.maximum(neg_lam, 0.0) + jnp.log1p(jnp.exp(-jnp.abs(neg_lam)))
    log_a = -LRU_C * r * softplus
    a = jnp.exp(log_a)
    u = jnp.sqrt(-jnp.tanh(log_a) * (a * a + 1.0)) * (gi * xc)

    row = lax.broadcasted_iota(jnp.int32, (ts, 1), 0) % SUBLANES
    d = 1
    while d < SUBLANES:
        keep = row >= d
        a_sh = jnp.where(keep, pltpu.roll(a, d, 0), 1.0)
        u_sh = jnp.where(keep, pltpu.roll(u, d, 0), 0.0)
        u = a * u_sh + u
        a = a * a_sh
        d *= 2
    a_s[...] = a
    u_s[...] = u

    def group(g, hprev):
        sl = pl.ds(pl.multiple_of(g * SUBLANES, SUBLANES), SUBLANES)
        hg = a_s[sl, :] * hprev + u_s[sl, :]
        h_s[sl, :] = hg
        return hg[SUBLANES - 1:SUBLANES, :]

    carry_s[...] = lax.fori_loop(0, ts // SUBLANES, group, carry_s[...], unroll=8)

    y = h_s[...] * jax.nn.gelu(yg_ref[0], approximate=True)
    ms = jnp.mean(jnp.square(y), axis=-1, keepdims=True)
    o_ref[0] = (y * lax.rsqrt(ms + LN_EPS) * g_ref[...]).astype(BF16)


def _lru_call(xr, yg, conv_w, conv_b, wa_bd, wx_bd, b_a, b_x, lam, g_lru):
    B, S, W = xr.shape
    ts = min(ROW_TILE, S)
    rows = pl.BlockSpec((1, ts, W), lambda b, i: (b, i, 0))
    vec = pl.BlockSpec((1, W), lambda b, i: (0, 0))
    mat = pl.BlockSpec((W, W), lambda b, i: (0, 0))
    return pl.pallas_call(
        _lru_body,
        grid=(B, S // ts),
        in_specs=[rows, rows, pl.BlockSpec((CONV_WIDTH, W), lambda b, i: (0, 0)), vec, mat, mat,
                  vec, vec, vec, vec],
        out_specs=rows,
        out_shape=jax.ShapeDtypeStruct((B, S, W), BF16),
        scratch_shapes=[pltpu.VMEM((ts + SUBLANES, W), F32),
                        pltpu.VMEM((ts, W), F32), pltpu.VMEM((ts, W), F32), pltpu.VMEM((ts, W), F32),
                        pltpu.VMEM((1, W), F32)],
        compiler_params=_cparams(2),
        name="rg_lru",
    )(xr, yg, conv_w, conv_b.reshape(1, W), wa_bd, wx_bd, b_a.reshape(1, W), b_x.reshape(1, W),
      lam.reshape(1, W), g_lru.reshape(1, W))


def _layer_norm(y, g, b):
    mu = jnp.mean(y, axis=-1, keepdims=True)
    yc = y - mu
    var = jnp.mean(jnp.square(yc), axis=-1, keepdims=True)
    return yc * lax.rsqrt(var + LN_EPS) * g + b


def _mlp_body(at_ref, lr_ref, x_ref, g1_ref, sh2_ref, sc2_ref, g2_ref, wo_ref, l1g_ref, l1b_ref,
              wu_ref, wd_ref, l2g_ref, l2b_ref, o_ref, *, alpha, ff_chunk):
    aw = at_ref.shape[2]
    mix = jnp.dot(at_ref[0], wo_ref[0:aw, :], preferred_element_type=F32)
    mix = mix + jnp.dot(lr_ref[0], wo_ref[aw:, :], preferred_element_type=F32)
    x1 = _layer_norm(alpha * x_ref[0] + (1.0 + g1_ref[0]) * mix, l1g_ref[...], l1b_ref[...])
    h2 = (x1 * (1.0 + sc2_ref[0]) + sh2_ref[0]).astype(BF16)
    d_ff = wu_ref.shape[1]
    ff = jnp.zeros(x1.shape, F32)
    for c in range(d_ff // ff_chunk):
        cs = slice(c * ff_chunk, (c + 1) * ff_chunk)
        up = jnp.dot(h2, wu_ref[:, cs], preferred_element_type=F32)
        act = jnp.square(jnp.maximum(up, 0.0)).astype(BF16)
        ff = ff + jnp.dot(act, wd_ref[cs, :], preferred_element_type=F32)
    o_ref[0] = _layer_norm(alpha * x1 + (1.0 + g2_ref[0]) * ff, l2g_ref[...], l2b_ref[...])


def _mlp_call(attn_n, lru_n, x, mod3, w_out, ln1_g, ln1_b, w_up, w_down, ln2_g, ln2_b, alpha):
    B, S, D = x.shape
    tm = min(ROW_TILE, S)
    d_ff = w_up.shape[1]
    half = pl.BlockSpec((1, tm, attn_n.shape[2]), lambda b, i: (b, i, 0))
    rows = pl.BlockSpec((1, tm, D), lambda b, i: (b, i, 0))
    modv = lambda j: pl.BlockSpec((1, 1, D), lambda b, i: (b, 0, j))
    vec = pl.BlockSpec((1, D), lambda b, i: (0, 0))
    wspec = lambda shape: pl.BlockSpec(shape, lambda b, i: (0, 0), pipeline_mode=pl.Buffered(1))
    return pl.pallas_call(
        functools.partial(_mlp_body, alpha=alpha, ff_chunk=min(1024, d_ff)),
        grid=(B, S // tm),
        in_specs=[half, half, rows, modv(2), modv(3), modv(4), modv(5),
                  wspec((D, D)), vec, vec, wspec((D, d_ff)), wspec((d_ff, D)), vec, vec],
        out_specs=rows,
        out_shape=jax.ShapeDtypeStruct((B, S, D), F32),
        compiler_params=_cparams(2),
        name="out_mlp",
    )(attn_n, lru_n, x, mod3, mod3, mod3, mod3, w_out, ln1_g.reshape(1, D), ln1_b.reshape(1, D),
      w_up, w_down, ln2_g.reshape(1, D), ln2_b.reshape(1, D))


def _prep_w_in(w):
    D = w.shape[0]
    o_ki = 3 * ATTN_WIDTH + IDX_WIDTH
    o_xr = o_ki + IDX_HEAD_DIM + IDX_HEADS
    pad = IN_COLS_PAD - (o_xr + 2 * LRU_WIDTH)
    return jnp.concatenate([w[:, :o_ki], w[:, o_xr:], w[:, o_ki:o_xr], jnp.zeros((D, pad), w.dtype)],
                           axis=1).astype(BF16)


def _block_diag(w):
    n, c, d = w.shape
    eye = jnp.eye(n, dtype=w.dtype)
    return (w[:, :, None, :] * eye[:, None, :, None]).reshape(n * c, n * d).astype(BF16)


def kernel(x, c, positions, w_mod, b_mod, w_in, conv_w, conv_b, w_gate_a, b_gate_a, w_gate_x, b_gate_x,
           lru_lambda, g_attn_out, g_lru_out, w_out, ln1_g, ln1_b, w_up, w_down, ln2_g, ln2_b):
    B, S, D = x.shape
    L = w_mod.shape[0]
    assert S % KEY_CHUNK == 0 and S % min(ROW_TILE, S) == 0 and D % LANES == 0
    assert w_in.shape[2] == 3 * ATTN_WIDTH + IDX_WIDTH + IDX_HEAD_DIM + IDX_HEADS + 2 * LRU_WIDTH
    alpha = (2.0 * L) ** 0.25

    mod = _mod_call(c, w_mod, b_mod)
    tabs = _rope_call(positions)
    for l in range(L):
        mod3 = mod[l].reshape(B, 1, N_MOD * D)
        qT, k, vT, qiT, kia, kib, wiT, xr, yg = _inproj_call(x, mod3, _prep_w_in(w_in[l]), tabs)
        attn_n = _dsa_call(qT, qiT, wiT, k, vT, kia, kib, g_attn_out[l])
        lru_n = _lru_call(xr, yg, conv_w[l], conv_b[l], _block_diag(w_gate_a[l]), _block_diag(w_gate_x[l]),
                          b_gate_a[l], b_gate_x[l], lru_lambda[l], g_lru_out[l])
        x = _mlp_call(attn_n, lru_n, x, mod3, w_out[l].astype(BF16), ln1_g[l], ln1_b[l],
                      w_up[l].astype(BF16), w_down[l].astype(BF16), ln2_g[l], ln2_b[l], alpha)
    return x
```

```python
import functools
import math

import jax
import jax.numpy as jnp
import numpy as np
from jax import lax
from jax.experimental import pallas as pl
from jax.experimental.pallas import tpu as pltpu

ATTN_HEAD_DIM = 128
ATTN_HEADS = 4
ATTN_WIDTH = ATTN_HEADS * ATTN_HEAD_DIM
IDX_HEADS = 8
IDX_HEAD_DIM = 64
IDX_WIDTH = IDX_HEADS * IDX_HEAD_DIM
TOPK_MAX = 256
LRU_WIDTH = 512
LRU_BLOCKS = 8
LRU_BLOCK = LRU_WIDTH // LRU_BLOCKS
CONV_WIDTH = 4
LRU_C = 8.0
ROPE_THETA = 10000.0
LN_EPS = 1e-5
N_MOD = 6

LANES = 128
SUBLANES = 8
MXU_COLS = 256
VMEM_LIMIT_BYTES = 56 * 1024 * 1024

KEY_CHUNK = 256
ROW_TILE = 512
IDX_SUB = 128
ATTN_SLAB = 32
SWEEP_GROUP = 4

SEG_Q, SEG_K, SEG_V, SEG_QI, SEG_XR, SEG_YG, SEG_KIW = 0, 512, 1024, 1536, 2048, 2560, 3072
IN_COLS_PAD = 3200

BF16 = jnp.bfloat16
F32 = jnp.float32
INT_MIN = -2 ** 31
KEY_NEG_INF = int(np.int32(np.uint32(0xFF800000) ^ np.uint32(0x7FFFFFFF)))


def _cparams(n_grid):
    return pltpu.CompilerParams(
        dimension_semantics=("arbitrary",) * n_grid,
        vmem_limit_bytes=VMEM_LIMIT_BYTES)


def _mod_body(c_ref, w_ref, b_ref, o_ref):
    c = c_ref[...]
    ca = (c * jax.nn.sigmoid(c)).astype(BF16)
    o_ref[0] = jnp.dot(ca, w_ref[0].astype(BF16), preferred_element_type=F32) + b_ref[0]


def _mod_call(c, w_mod, b_mod):
    L, D, N = w_mod.shape
    B = c.shape[0]
    tn = 2048 if N % 2048 == 0 else N
    return pl.pallas_call(
        _mod_body,
        grid=(L, N // tn),
        in_specs=[pl.BlockSpec((B, D), lambda l, j: (0, 0)),
                  pl.BlockSpec((1, D, tn), lambda l, j: (l, 0, j)),
                  pl.BlockSpec((1, 1, tn), lambda l, j: (l, 0, j))],
        out_specs=pl.BlockSpec((1, B, tn), lambda l, j: (l, 0, j)),
        out_shape=jax.ShapeDtypeStruct((L, B, N), F32),
        compiler_params=_cparams(2),
        name="adaln_mod",
    )(c, w_mod, b_mod.reshape(L, 1, N))


def _rope_body(pos_ref, inv_a_ref, sgn_a_ref, inv_i_ref, sgn_i_ref, ca_ref, sa_ref, ci_ref, si_ref):
    pos = pos_ref[0].astype(F32)
    ang_a = inv_a_ref[...] * pos
    ang_i = inv_i_ref[...] * pos
    ca_ref[0] = jnp.cos(ang_a).T
    sa_ref[0] = (jnp.sin(ang_a) * sgn_a_ref[...]).T
    ci_ref[0] = jnp.cos(ang_i).T
    si_ref[0] = (jnp.sin(ang_i) * sgn_i_ref[...]).T


def _rope_call(positions):
    B, S = positions.shape
    ts = min(ROW_TILE, S)
    half_a, half_i = ATTN_HEAD_DIM // 2, IDX_HEAD_DIM // 2
    inv_a = ROPE_THETA ** (-jnp.arange(0, ATTN_HEAD_DIM, 2, dtype=F32) / ATTN_HEAD_DIM)
    inv_i = ROPE_THETA ** (-jnp.arange(0, IDX_HEAD_DIM, 2, dtype=F32) / IDX_HEAD_DIM)
    inv_a = jnp.tile(inv_a, LANES // half_a).reshape(LANES, 1)
    inv_i = jnp.tile(inv_i, LANES // half_i).reshape(LANES, 1)
    lane = np.arange(LANES)
    sgn_a = jnp.asarray(np.where(lane % ATTN_HEAD_DIM < half_a, -1.0, 1.0).astype(np.float32)).reshape(LANES, 1)
    sgn_i = jnp.asarray(np.where(lane % IDX_HEAD_DIM < half_i, -1.0, 1.0).astype(np.float32)).reshape(LANES, 1)
    col = pl.BlockSpec((LANES, 1), lambda b, i: (0, 0))
    tab = pl.BlockSpec((1, ts, LANES), lambda b, i: (b, i, 0))
    shp = jax.ShapeDtypeStruct((B, S, LANES), F32)
    return pl.pallas_call(
        _rope_body,
        grid=(B, S // ts),
        in_specs=[pl.BlockSpec((1, 1, ts), lambda b, i: (b, 0, i)), col, col, col, col],
        out_specs=[tab, tab, tab, tab],
        out_shape=[shp, shp, shp, shp],
        compiler_params=_cparams(2),
        name="rope_tables",
    )(positions.reshape(B, 1, S), inv_a, sgn_a, inv_i, sgn_i)


def _rope_a(x, cos, sin_signed):
    return x * cos + pltpu.roll(x, ATTN_HEAD_DIM // 2, 1) * sin_signed


def _rope_i(x, cos, sin_signed, lo_half):
    half = IDX_HEAD_DIM // 2
    partner = jnp.where(lo_half, pltpu.roll(x, LANES - half, 1), pltpu.roll(x, half, 1))
    return x * cos + partner * sin_signed


def _inproj_body(x_ref, sh_ref, sc_ref, w_ref, ca_ref, sa_ref, ci_ref, si_ref,
                 qT_ref, k_ref, vT_ref, qiT_ref, kia_ref, kib_ref, wiT_ref, xr_ref, yg_ref):
    tm = x_ref.shape[1]
    n_chunk = tm // KEY_CHUNK
    h = (x_ref[0] * (1.0 + sc_ref[0]) + sh_ref[0]).astype(BF16)
    ca, sa, ci, si = ca_ref[0], sa_ref[0], ci_ref[0], si_ref[0]
    lane = lax.broadcasted_iota(jnp.int32, (tm, LANES), 1)
    lo_half = (lane % IDX_HEAD_DIM) < (IDX_HEAD_DIM // 2)
    q_scale = ATTN_HEAD_DIM ** -0.5 * math.log2(math.e)

    def seg(start, width=LANES):
        return jnp.dot(h, w_ref[:, start:start + width], preferred_element_type=F32)

    def store_T(ref, g, val):
        vt = val.T.astype(BF16)
        for c in range(n_chunk):
            ref[0, c, g * LANES:(g + 1) * LANES, :] = vt[:, c * KEY_CHUNK:(c + 1) * KEY_CHUNK]

    def lane_groups(start):
        for pair in range(ATTN_WIDTH // MXU_COLS):
            wide = seg(start + pair * MXU_COLS, MXU_COLS)
            for half in range(MXU_COLS // LANES):
                yield pair * (MXU_COLS // LANES) + half, wide[:, half * LANES:(half + 1) * LANES]

    for g, val in lane_groups(SEG_Q):
        store_T(qT_ref, g, _rope_a(val, ca, sa) * q_scale)
    for g, val in lane_groups(SEG_K):
        k_ref[0, :, g * LANES:(g + 1) * LANES] = _rope_a(val, ca, sa).astype(BF16)
    for g, val in lane_groups(SEG_V):
        store_T(vT_ref, g, val)
    for g, val in lane_groups(SEG_QI):
        store_T(qiT_ref, g, _rope_i(val, ci, si, lo_half))
    xr_ref[0] = seg(SEG_XR, LRU_WIDTH)
    yg_ref[0] = seg(SEG_YG, LRU_WIDTH)
    kiw = seg(SEG_KIW)
    ki = jnp.where(lane < IDX_HEAD_DIM, _rope_i(kiw, ci, si, lo_half), 0.0)
    kia_ref[0] = ki.astype(BF16)
    kib_ref[0] = pltpu.roll(ki, IDX_HEAD_DIM, 1).astype(BF16)
    wiT = kiw.T[IDX_HEAD_DIM:IDX_HEAD_DIM + IDX_HEADS, :] * (IDX_HEADS ** -0.5)
    for c in range(n_chunk):
        wiT_ref[0, c] = wiT[:, c * KEY_CHUNK:(c + 1) * KEY_CHUNK]


def _inproj_call(x, mod3, w_in_p, tabs):
    B, S, D = x.shape
    tm = min(ROW_TILE, S)
    nc = S // KEY_CHUNK
    cpb = tm // KEY_CHUNK
    ca, sa, ci, si = tabs
    tab = pl.BlockSpec((1, tm, LANES), lambda b, i: (b, i, 0))
    tspec = pl.BlockSpec((1, cpb, ATTN_WIDTH, KEY_CHUNK), lambda b, i: (b, i, 0, 0))
    tshape = jax.ShapeDtypeStruct((B, nc, ATTN_WIDTH, KEY_CHUNK), BF16)
    rows = lambda w, dt: (pl.BlockSpec((1, tm, w), lambda b, i: (b, i, 0)), jax.ShapeDtypeStruct((B, S, w), dt))
    k_spec, k_shape = rows(ATTN_WIDTH, BF16)
    ki_spec, ki_shape = rows(LANES, BF16)
    r_spec, r_shape = rows(LRU_WIDTH, F32)
    return pl.pallas_call(
        _inproj_body,
        grid=(B, S // tm),
        in_specs=[pl.BlockSpec((1, tm, D), lambda b, i: (b, i, 0)),
                  pl.BlockSpec((1, 1, D), lambda b, i: (b, 0, 0)),
                  pl.BlockSpec((1, 1, D), lambda b, i: (b, 0, 1)),
                  pl.BlockSpec((D, IN_COLS_PAD), lambda b, i: (0, 0), pipeline_mode=pl.Buffered(1)),
                  tab, tab, tab, tab],
        out_specs=[tspec, k_spec, tspec, tspec, ki_spec, ki_spec,
                   pl.BlockSpec((1, cpb, IDX_HEADS, KEY_CHUNK), lambda b, i: (b, i, 0, 0)),
                   r_spec, r_spec],
        out_shape=[tshape, k_shape, tshape, tshape, ki_shape, ki_shape,
                   jax.ShapeDtypeStruct((B, nc, IDX_HEADS, KEY_CHUNK), F32),
                   r_shape, r_shape],
        compiler_params=_cparams(2),
        name="in_proj",
    )(x, mod3, mod3, w_in_p, ca, sa, ci, si)


_BUTTERFLY_MASK = {16: 0x0000FFFF, 8: 0x00FF00FF, 4: 0x0F0F0F0F, 2: 0x33333333, 1: 0x55555555}


def _butterfly(lo, hi, dist):
    t = (lo ^ lax.shift_right_logical(hi, jnp.int32(dist))) & jnp.int32(_BUTTERFLY_MASK[dist])
    return lo ^ t, hi ^ jnp.left_shift(t, jnp.int32(dist))


def _bit_transpose32(load, store):
    for k in range(8):
        a, b, c, d = load(k), load(k + 8), load(k + 16), load(k + 24)
        a, c = _butterfly(a, c, 16)
        b, d = _butterfly(b, d, 16)
        a, b = _butterfly(a, b, 8)
        c, d = _butterfly(c, d, 8)
        store(k, a), store(k + 8, b), store(k + 16, c), store(k + 24, d)
    for g in range(0, 32, 8):
        w = [store(g + x) for x in range(8)]
        for dist in (4, 2, 1):
            for x in range(8):
                if not x & dist:
                    w[x], w[x + dist] = _butterfly(w[x], w[x + dist], dist)
        for x in range(8):
            store(g + x, w[x])


def _fold_rows(x, op):
    r, c = x.shape
    x = x.reshape(r // SUBLANES, SUBLANES, c)
    return op(x, axis=0)


def _dsa_body(qT_ref, qiT_ref, wiT_ref, k_ref, vT_ref, kia_ref, kib_ref, g_ref, o_ref,
              key_s, plane_s, alive_s, bias_s, sc0_s, sc1_s, s0_s, s1_s, p_s, m_s, l_s, acc_s,
              *, topk, idx_bits, group):
    sc_s = (sc0_s, sc1_s)
    s_s = (s0_s, s1_s)
    i = pl.program_id(1)
    tq = KEY_CHUNK
    tk = KEY_CHUNK
    n_chunks = i + 1
    q_pos = i * tq + lax.broadcasted_iota(jnp.int32, (1, tq), 1)

    slabs = [slice(r, r + ATTN_SLAB) for r in range(0, tk, ATTN_SLAB)]

    def pipelined(produce, consume):
        produce(0, 0)

        def pair(t, carry):
            c = 2 * t
            produce(c + 1, 1)
            consume(c, 0, False)
            produce(c + 2, 0)
            consume(c + 1, 1, False)
            return carry
        lax.fori_loop(0, i // 2, pair, 0)

        @pl.when(i % 2 == 0)
        def _():
            consume(i, 0, True)

        @pl.when(i % 2 == 1)
        def _():
            produce(i, 1)
            consume(i - 1, 0, False)
            consume(i, 1, True)

    def score_matmuls(c, buf):
        r0 = pl.multiple_of(c * tk, tk)
        ka = kia_ref[0, pl.ds(r0, tk), :]
        kb = kib_ref[0, pl.ds(r0, tk), :]
        for p in range(IDX_HEADS // 2):
            qp = qiT_ref[0, 0, p * LANES:(p + 1) * LANES, :]
            sc_s[buf][2 * p] = jnp.dot(ka, qp, preferred_element_type=F32)
            sc_s[buf][2 * p + 1] = jnp.dot(kb, qp, preferred_element_type=F32)

    def score_keys(c, buf, diagonal):
        c0 = pl.multiple_of(c * tk, tk)
        for sl in slabs:
            acc = wiT_ref[0, 0, 0:1, :] * jnp.maximum(sc_s[buf][0, sl, :], 0.0)
            for hd in range(1, IDX_HEADS):
                acc = acc + wiT_ref[0, 0, hd:hd + 1, :] * jnp.maximum(sc_s[buf][hd, sl, :], 0.0)
            acc = acc + 0.0
            bits = pltpu.bitcast(acc, jnp.int32)
            mono = bits ^ ((bits >> 31) & 0x7FFFFFFF)
            if diagonal:
                k_pos = c0 + sl.start + lax.broadcasted_iota(jnp.int32, (ATTN_SLAB, 1), 0)
                mono = jnp.where(k_pos <= q_pos, mono, KEY_NEG_INF)
            key_s[pl.ds(c0 + sl.start, ATTN_SLAB), :] = mono
        for lh in range(tq // LANES):
            ls = slice(lh * LANES, (lh + 1) * LANES)
            def key_word(r, ls=ls):
                return key_s[pl.ds(c0 + SUBLANES * r, SUBLANES), ls]

            def plane_word(b, val=None, ls=ls):
                if val is None:
                    return plane_s[c, b, :, ls]
                plane_s[c, b, :, ls] = val

            _bit_transpose32(key_word, plane_word)
            plane_s[c, 0, :, ls] = ~plane_s[c, 0, :, ls]

    pipelined(score_matmuls, score_keys)

    n_groups = (n_chunks + group - 1) // group

    def pad_chunk(c, carry):
        plane_s[c] = jnp.zeros((32, SUBLANES, tq), jnp.int32)
        alive_s[c] = jnp.zeros((SUBLANES, tq), jnp.int32)
        return carry
    lax.fori_loop(n_chunks, n_groups * group, pad_chunk, 0)

    def sweep(j, keep_mask):
        def grp(g, part):
            for cc in range(group):
                c = g * group + cc
                alive = alive_s[c]
                if keep_mask is not None:
                    alive = alive & (plane_s[c, j - 1] ^ keep_mask)
                    alive_s[c] = alive
                part = part + lax.population_count(alive & plane_s[c, j])
            return part
        part = lax.fori_loop(0, n_groups, grp, jnp.zeros((SUBLANES, tq), jnp.int32))
        return part.sum(axis=0, keepdims=True)

    def decide(j, c1, state):
        thr_u, n_gt, n_alive = state
        ok = n_gt + c1 >= topk
        thr_u = thr_u | jnp.where(ok, jnp.left_shift(jnp.int32(1), 31 - j), 0)
        return jnp.where(ok, 0, -1), (thr_u, jnp.where(ok, n_gt, n_gt + c1), jnp.where(ok, c1, n_alive - c1))

    def init_alive(c, carry):
        alive_s[c] = jnp.full((SUBLANES, tq), -1, jnp.int32)
        return carry
    lax.fori_loop(0, n_chunks, init_alive, 0)
    zero = jnp.zeros((1, tq), jnp.int32)
    flip, state = decide(0, sweep(0, None), (zero, zero, zero + n_chunks * tk))

    def bit_step(j, carry):
        flip_prev, state = carry
        return decide(j, sweep(j, jnp.broadcast_to(flip_prev, (SUBLANES, tq))), state)

    flip_last, (thr_u, n_gt, n_alive) = lax.fori_loop(1, 32, bit_step, (flip, state))
    thr = thr_u ^ INT_MIN
    cnt_thr = n_gt + n_alive

    @pl.when(jnp.max(cnt_thr) > topk)
    def _():
        need = topk - n_gt
        def row_idx(c):
            return c * tk + lax.broadcasted_iota(jnp.int32, (tk, 1), 0)
        flip8 = jnp.broadcast_to(flip_last, (SUBLANES, tq))
        def settle(c, carry):
            alive_s[c] = alive_s[c] & (plane_s[c, 31] ^ flip8)
            return carry
        lax.fori_loop(0, n_chunks, settle, 0)
        def count_tied_below(cand):
            def grp(g, part):
                for cc in range(group):
                    c = g * group + cc
                    first_row = c * tk + lax.broadcasted_iota(jnp.int32, (SUBLANES, 1), 0)
                    n_below = jnp.clip((cand - first_row + (SUBLANES - 1)) >> 3, 0, 32)
                    top_bits = jnp.where(n_below > 0,
                                         jnp.left_shift(jnp.int32(-1), 32 - jnp.maximum(n_below, 1)), 0)
                    part = part + lax.population_count(alive_s[c] & top_bits)
                return part
            part = lax.fori_loop(0, n_groups, grp, jnp.zeros((SUBLANES, tq), jnp.int32))
            return part.sum(axis=0, keepdims=True)
        def idx_step(j, cut):
            cand = cut | jnp.left_shift(jnp.int32(1), idx_bits - 1 - j)
            return jnp.where(count_tied_below(cand) < need, cand, cut)
        cut = lax.fori_loop(0, idx_bits, idx_step, jnp.zeros((1, tq), jnp.int32))
        fix = cnt_thr > topk
        def rewrite(c, carry):
            sl = pl.ds(pl.multiple_of(c * tk, tk), tk)
            blk = key_s[sl, :]
            drop = (blk == thr) & (row_idx(c) > cut) & fix
            key_s[sl, :] = jnp.where(drop, blk - 1, blk)
            return carry
        lax.fori_loop(0, n_chunks, rewrite, 0)

    m_s[...] = jnp.full(m_s.shape, -jnp.inf, F32)
    l_s[...] = jnp.zeros(l_s.shape, F32)
    acc_s[...] = jnp.zeros(acc_s.shape, F32)

    heads = [slice(hd * ATTN_HEAD_DIM, (hd + 1) * ATTN_HEAD_DIM) for hd in range(ATTN_HEADS)]

    def logit_matmuls(c, buf):
        r0 = pl.multiple_of(c * tk, tk)
        for hd, hs in enumerate(heads):
            s_s[buf][hd] = jnp.dot(k_ref[0, pl.ds(r0, tk), hs], qT_ref[0, 0, hs, :],
                                   preferred_element_type=F32)

    def attn_chunk(c, buf, diagonal):
        r0 = pl.multiple_of(c * tk, tk)
        logits = s_s[buf]
        for sl in slabs:
            sel = key_s[pl.ds(r0 + sl.start, ATTN_SLAB), :] >= thr
            if diagonal:
                k_pos = r0 + sl.start + lax.broadcasted_iota(jnp.int32, (ATTN_SLAB, 1), 0)
                sel = sel & (k_pos <= q_pos)
            bias_s[sl, :] = jnp.where(sel, 0.0, -jnp.inf)
        m_safe, alpha = [], []
        for hd in range(ATTN_HEADS):
            m_old = m_s[hd]
            mx = None
            for sl in slabs:
                masked = logits[hd, sl, :] + bias_s[sl, :]
                logits[hd, sl, :] = masked
                fold = _fold_rows(masked, jnp.max)
                mx = fold if mx is None else jnp.maximum(mx, fold)
            m_new = jnp.maximum(m_old, mx.max(axis=0, keepdims=True))
            m_s[hd] = m_new
            m_safe.append(jnp.where(m_new == -jnp.inf, 0.0, m_new))
            alpha.append(jnp.exp2(m_old - m_safe[hd]))
        for hd in range(ATTN_HEADS):
            psum = jnp.zeros((SUBLANES, tq), F32)
            for sl in slabs:
                p = jnp.exp2(logits[hd, sl, :] - m_safe[hd])
                psum = psum + _fold_rows(p, jnp.sum)
                p_s[hd, sl, :] = p.astype(BF16)
            l_s[hd] = alpha[hd] * l_s[hd] + psum.sum(axis=0, keepdims=True)
        for hd, hs in enumerate(heads):
            acc_s[hd] = alpha[hd] * acc_s[hd] + jnp.dot(vT_ref[0, c, hs, :], p_s[hd],
                                                        preferred_element_type=F32)

    pipelined(logit_matmuls, attn_chunk)

    outT = jnp.concatenate([acc_s[hd] / l_s[hd] for hd in range(ATTN_HEADS)], axis=0)
    out = outT.T
    ms = jnp.mean(jnp.square(out), axis=-1, keepdims=True)
    o_ref[0] = (out * lax.rsqrt(ms + LN_EPS) * g_ref[...]).astype(BF16)


def _dsa_call(qT, qiT, wiT, k, vT, kia, kib, g_attn):
    B, S, _ = k.shape
    nc = S // KEY_CHUNK
    topk = min(TOPK_MAX, S // 4)
    idx_bits = max(1, int(math.ceil(math.log2(S))))
    group = math.gcd(SWEEP_GROUP, nc)
    blk_q = pl.BlockSpec((1, 1, ATTN_WIDTH, KEY_CHUNK), lambda b, i: (b, i, 0, 0))
    full = lambda shape: pl.BlockSpec(shape, lambda b, i: (b,) + (0,) * (len(shape) - 1))
    return pl.pallas_call(
        functools.partial(_dsa_body, topk=topk, idx_bits=idx_bits, group=group),
        grid=(B, nc),
        in_specs=[blk_q, blk_q,
                  pl.BlockSpec((1, 1, IDX_HEADS, KEY_CHUNK), lambda b, i: (b, i, 0, 0)),
                  full((1, S, ATTN_WIDTH)),
                  full((1, nc, ATTN_WIDTH, KEY_CHUNK)),
                  full((1, S, LANES)), full((1, S, LANES)),
                  pl.BlockSpec((1, ATTN_WIDTH), lambda b, i: (0, 0))],
        out_specs=pl.BlockSpec((1, KEY_CHUNK, ATTN_WIDTH), lambda b, i: (b, i, 0)),
        out_shape=jax.ShapeDtypeStruct((B, S, ATTN_WIDTH), BF16),
        scratch_shapes=[pltpu.VMEM((S, KEY_CHUNK), jnp.int32),
                        pltpu.VMEM((nc, 32, SUBLANES, KEY_CHUNK), jnp.int32),
                        pltpu.VMEM((nc, SUBLANES, KEY_CHUNK), jnp.int32),
                        pltpu.VMEM((KEY_CHUNK, KEY_CHUNK), F32),
                        pltpu.VMEM((IDX_HEADS, KEY_CHUNK, KEY_CHUNK), F32),
                        pltpu.VMEM((IDX_HEADS, KEY_CHUNK, KEY_CHUNK), F32),
                        pltpu.VMEM((ATTN_HEADS, KEY_CHUNK, KEY_CHUNK), F32),
                        pltpu.VMEM((ATTN_HEADS, KEY_CHUNK, KEY_CHUNK), F32),
                        pltpu.VMEM((ATTN_HEADS, KEY_CHUNK, KEY_CHUNK), BF16),
                        pltpu.VMEM((ATTN_HEADS, 1, KEY_CHUNK), F32),
                        pltpu.VMEM((ATTN_HEADS, 1, KEY_CHUNK), F32),
                        pltpu.VMEM((ATTN_HEADS, ATTN_HEAD_DIM, KEY_CHUNK), F32)],
        compiler_params=_cparams(2),
        name="dsa_attention",
    )(qT, qiT, wiT, k, vT, kia, kib, g_attn.reshape(1, ATTN_WIDTH))


def _lru_body(xr_ref, yg_ref, cw_ref, cb_ref, wa_ref, wx_ref, ba_ref, bx_ref, lam_ref, g_ref, o_ref,
              xbuf_s, a_s, u_s, h_s, carry_s):
    ts = xr_ref.shape[1]
    W = xr_ref.shape[2]
    pad = SUBLANES

    @pl.when(pl.program_id(1) == 0)
    def _():
        xbuf_s[0:pad, :] = jnp.zeros((pad, W), F32)
        carry_s[...] = jnp.zeros(carry_s.shape, F32)

    xbuf_s[pad:pad + ts, :] = xr_ref[0]
    xc = cb_ref[...]
    for j in range(CONV_WIDTH):
        xc = xc + xbuf_s[pad - (CONV_WIDTH - 1) + j:pad - (CONV_WIDTH - 1) + j + ts, :] * cw_ref[j:j + 1, :]
    xbuf_s[0:pad, :] = xbuf_s[ts:ts + pad, :]

    xcb = xc.astype(BF16)
    r = jax.nn.sigmoid(jnp.dot(xcb, wa_ref[...], preferred_element_type=F32) + ba_ref[...])
    gi = jax.nn.sigmoid(jnp.dot(xcb, wx_ref[...], preferred_element_type=F32) + bx_ref[...])
    neg_lam = -lam_ref[...]
    softplus = jnp.maximum(neg_lam, 0.0) + jnp.log1p(jnp.exp(-jnp.abs(neg_lam)))
    log_a = -LRU_C * r * softplus
    a = jnp.exp(log_a)
    u = jnp.sqrt(-jnp.tanh(log_a) * (a * a + 1.0)) * (gi * xc)

    row = lax.broadcasted_iota(jnp.int32, (ts, 1), 0) % SUBLANES
    d = 1
    while d < SUBLANES:
        keep = row >= d
        a_sh = jnp.where(keep, pltpu.roll(a, d, 0), 1.0)
        u_sh = jnp.where(keep, pltpu.roll(u, d, 0), 0.0)
        u = a * u_sh + u
        a = a * a_sh
        d *= 2
    a_s[...] = a
    u_s[...] = u

    def group(g, hprev):
        sl = pl.ds(pl.multiple_of(g * SUBLANES, SUBLANES), SUBLANES)
        hg = a_s[sl, :] * hprev + u_s[sl, :]
        h_s[sl, :] = hg
        return hg[SUBLANES - 1:SUBLANES, :]

    carry_s[...] = lax.fori_loop(0, ts // SUBLANES, group, carry_s[...], unroll=8)

    y = h_s[...] * jax.nn.gelu(yg_ref[0], approximate=True)
    ms = jnp.mean(jnp.square(y), axis=-1, keepdims=True)
    o_ref[0] = (y * lax.rsqrt(ms + LN_EPS) * g_ref[...]).astype(BF16)


def _lru_call(xr, yg, conv_w, conv_b, wa_bd, wx_bd, b_a, b_x, lam, g_lru):
    B, S, W = xr.shape
    ts = min(ROW_TILE, S)
    rows = pl.BlockSpec((1, ts, W), lambda b, i: (b, i, 0))
    vec = pl.BlockSpec((1, W), lambda b, i: (0, 0))
    mat = pl.BlockSpec((W, W), lambda b, i: (0, 0))
    return pl.pallas_call(
        _lru_body,
        grid=(B, S // ts),
        in_specs=[rows, rows, pl.BlockSpec((CONV_WIDTH, W), lambda b, i: (0, 0)), vec, mat, mat,
                  vec, vec, vec, vec],
        out_specs=rows,
        out_shape=jax.ShapeDtypeStruct((B, S, W), BF16),
        scratch_shapes=[pltpu.VMEM((ts + SUBLANES, W), F32),
                        pltpu.VMEM((ts, W), F32), pltpu.VMEM((ts, W), F32), pltpu.VMEM((ts, W), F32),
                        pltpu.VMEM((1, W), F32)],
        compiler_params=_cparams(2),
        name="rg_lru",
    )(xr, yg, conv_w, conv_b.reshape(1, W), wa_bd, wx_bd, b_a.reshape(1, W), b_x.reshape(1, W),
      lam.reshape(1, W), g_lru.reshape(1, W))


def _layer_norm(y, g, b):
    mu = jnp.mean(y, axis=-1, keepdims=True)
    yc = y - mu
    var = jnp.mean(jnp.square(yc), axis=-1, keepdims=True)
    return yc * lax.rsqrt(var + LN_EPS) * g + b


def _mlp_body(at_ref, lr_ref, x_ref, g1_ref, sh2_ref, sc2_ref, g2_ref, wo_ref, l1g_ref, l1b_ref,
              wu_ref, wd_ref, l2g_ref, l2b_ref, o_ref, *, alpha, ff_chunk):
    aw = at_ref.shape[2]
    mix = jnp.dot(at_ref[0], wo_ref[0:aw, :], preferred_element_type=F32)
    mix = mix + jnp.dot(lr_ref[0], wo_ref[aw:, :], preferred_element_type=F32)
    x1 = _layer_norm(alpha * x_ref[0] + (1.0 + g1_ref[0]) * mix, l1g_ref[...], l1b_ref[...])
    h2 = (x1 * (1.0 + sc2_ref[0]) + sh2_ref[0]).astype(BF16)
    d_ff = wu_ref.shape[1]
    ff = jnp.zeros(x1.shape, F32)
    for c in range(d_ff // ff_chunk):
        cs = slice(c * ff_chunk, (c + 1) * ff_chunk)
        up = jnp.dot(h2, wu_ref[:, cs], preferred_element_type=F32)
        act = jnp.square(jnp.maximum(up, 0.0)).astype(BF16)
        ff = ff + jnp.dot(act, wd_ref[cs, :], preferred_element_type=F32)
    o_ref[0] = _layer_norm(alpha * x1 + (1.0 + g2_ref[0]) * ff, l2g_ref[...], l2b_ref[...])


def _mlp_call(attn_n, lru_n, x, mod3, w_out, ln1_g, ln1_b, w_up, w_down, ln2_g, ln2_b, alpha):
    B, S, D = x.shape
    tm = min(ROW_TILE, S)
    d_ff = w_up.shape[1]
    half = pl.BlockSpec((1, tm, attn_n.shape[2]), lambda b, i: (b, i, 0))
    rows = pl.BlockSpec((1, tm, D), lambda b, i: (b, i, 0))
    modv = lambda j: pl.BlockSpec((1, 1, D), lambda b, i: (b, 0, j))
    vec = pl.BlockSpec((1, D), lambda b, i: (0, 0))
    wspec = lambda shape: pl.BlockSpec(shape, lambda b, i: (0, 0), pipeline_mode=pl.Buffered(1))
    return pl.pallas_call(
        functools.partial(_mlp_body, alpha=alpha, ff_chunk=min(1024, d_ff)),
        grid=(B, S // tm),
        in_specs=[half, half, rows, modv(2), modv(3), modv(4), modv(5),
                  wspec((D, D)), vec, vec, wspec((D, d_ff)), wspec((d_ff, D)), vec, vec],
        out_specs=rows,
        out_shape=jax.ShapeDtypeStruct((B, S, D), F32),
        compiler_params=_cparams(2),
        name="out_mlp",
    )(attn_n, lru_n, x, mod3, mod3, mod3, mod3, w_out, ln1_g.reshape(1, D), ln1_b.reshape(1, D),
      w_up, w_down, ln2_g.reshape(1, D), ln2_b.reshape(1, D))


def _prep_w_in(w):
    D = w.shape[0]
    o_ki = 3 * ATTN_WIDTH + IDX_WIDTH
    o_xr = o_ki + IDX_HEAD_DIM + IDX_HEADS
    pad = IN_COLS_PAD - (o_xr + 2 * LRU_WIDTH)
    return jnp.concatenate([w[:, :o_ki], w[:, o_xr:], w[:, o_ki:o_xr], jnp.zeros((D, pad), w.dtype)],
                           axis=1).astype(BF16)


def _block_diag(w):
    n, c, d = w.shape
    eye = jnp.eye(n, dtype=w.dtype)
    return (w[:, :, None, :] * eye[:, None, :, None]).reshape(n * c, n * d).astype(BF16)


def kernel(x, c, positions, w_mod, b_mod, w_in, conv_w, conv_b, w_gate_a, b_gate_a, w_gate_x, b_gate_x,
           lru_lambda, g_attn_out, g_lru_out, w_out, ln1_g, ln1_b, w_up, w_down, ln2_g, ln2_b):
    B, S, D = x.shape
    L = w_mod.shape[0]
    assert S % KEY_CHUNK == 0 and S % min(ROW_TILE, S) == 0 and D % LANES == 0
    assert w_in.shape[2] == 3 * ATTN_WIDTH + IDX_WIDTH + IDX_HEAD_DIM + IDX_HEADS + 2 * LRU_WIDTH
    alpha = (2.0 * L) ** 0.25

    mod = _mod_call(c, w_mod, b_mod)
    tabs = _rope_call(positions)
    for l in range(L):
        mod3 = mod[l].reshape(B, 1, N_MOD * D)
        qT, k, vT, qiT, kia, kib, wiT, xr, yg = _inproj_call(x, mod3, _prep_w_in(w_in[l]), tabs)
        attn_n = _dsa_call(qT, qiT, wiT, k, vT, kia, kib, g_attn_out[l])
        lru_n = _lru_call(xr, yg, conv_w[l], conv_b[l], _block_diag(w_gate_a[l]), _block_diag(w_gate_x[l]),
                          b_gate_a[l], b_gate_x[l], lru_lambda[l], g_lru_out[l])
        x = _mlp_call(attn_n, lru_n, x, mod3, w_out[l].astype(BF16), ln1_g[l], ln1_b[l],
                      w_up[l].astype(BF16), w_down[l].astype(BF16), ln2_g[l], ln2_b[l], alpha)
    return x
```

```python
import functools
import math

import jax
import jax.numpy as jnp
import numpy as np
from jax import lax
from jax.experimental import pallas as pl
from jax.experimental.pallas import tpu as pltpu

ATTN_HEAD_DIM = 128
ATTN_HEADS = 4
ATTN_WIDTH = ATTN_HEADS * ATTN_HEAD_DIM
IDX_HEADS = 8
IDX_HEAD_DIM = 64
IDX_WIDTH = IDX_HEADS * IDX_HEAD_DIM
TOPK_MAX = 256
LRU_WIDTH = 512
LRU_BLOCKS = 8
LRU_BLOCK = LRU_WIDTH // LRU_BLOCKS
CONV_WIDTH = 4
LRU_C = 8.0
ROPE_THETA = 10000.0
LN_EPS = 1e-5
N_MOD = 6

LANES = 128
SUBLANES = 8
MXU_COLS = 256
VMEM_LIMIT_BYTES = 56 * 1024 * 1024

KEY_CHUNK = 256
ROW_TILE = 512
IDX_SUB = 128
ATTN_SLAB = 32
SWEEP_GROUP = 4
HEAD_PITCH = KEY_CHUNK + SUBLANES

SEG_Q, SEG_K, SEG_V, SEG_QI, SEG_XR, SEG_YG, SEG_KIW = 0, 512, 1024, 1536, 2048, 2560, 3072
IN_COLS_PAD = 3200

BF16 = jnp.bfloat16
F32 = jnp.float32
INT_MIN = -2 ** 31
KEY_NEG_INF = int(np.int32(np.uint32(0xFF800000) ^ np.uint32(0x7FFFFFFF)))


def _cparams(n_grid):
    return pltpu.CompilerParams(
        dimension_semantics=("arbitrary",) * n_grid,
        vmem_limit_bytes=VMEM_LIMIT_BYTES)


def _mod_body(c_ref, w_ref, b_ref, o_ref):
    c = c_ref[...]
    ca = (c * jax.nn.sigmoid(c)).astype(BF16)
    o_ref[0] = jnp.dot(ca, w_ref[0].astype(BF16), preferred_element_type=F32) + b_ref[0]


def _mod_call(c, w_mod, b_mod):
    L, D, N = w_mod.shape
    B = c.shape[0]
    tn = 2048 if N % 2048 == 0 else N
    return pl.pallas_call(
        _mod_body,
        grid=(L, N // tn),
        in_specs=[pl.BlockSpec((B, D), lambda l, j: (0, 0)),
                  pl.BlockSpec((1, D, tn), lambda l, j: (l, 0, j)),
                  pl.BlockSpec((1, 1, tn), lambda l, j: (l, 0, j))],
        out_specs=pl.BlockSpec((1, B, tn), lambda l, j: (l, 0, j)),
        out_shape=jax.ShapeDtypeStruct((L, B, N), F32),
        compiler_params=_cparams(2),
        name="adaln_mod",
    )(c, w_mod, b_mod.reshape(L, 1, N))


def _rope_body(pos_ref, inv_a_ref, sgn_a_ref, inv_i_ref, sgn_i_ref, ca_ref, sa_ref, ci_ref, si_ref):
    pos = pos_ref[0].astype(F32)
    ang_a = inv_a_ref[...] * pos
    ang_i = inv_i_ref[...] * pos
    ca_ref[0] = jnp.cos(ang_a).T
    sa_ref[0] = (jnp.sin(ang_a) * sgn_a_ref[...]).T
    ci_ref[0] = jnp.cos(ang_i).T
    si_ref[0] = (jnp.sin(ang_i) * sgn_i_ref[...]).T


def _rope_call(positions):
    B, S = positions.shape
    ts = min(ROW_TILE, S)
    half_a, half_i = ATTN_HEAD_DIM // 2, IDX_HEAD_DIM // 2
    inv_a = ROPE_THETA ** (-jnp.arange(0, ATTN_HEAD_DIM, 2, dtype=F32) / ATTN_HEAD_DIM)
    inv_i = ROPE_THETA ** (-jnp.arange(0, IDX_HEAD_DIM, 2, dtype=F32) / IDX_HEAD_DIM)
    inv_a = jnp.tile(inv_a, LANES // half_a).reshape(LANES, 1)
    inv_i = jnp.tile(inv_i, LANES // half_i).reshape(LANES, 1)
    lane = np.arange(LANES)
    sgn_a = jnp.asarray(np.where(lane % ATTN_HEAD_DIM < half_a, -1.0, 1.0).astype(np.float32)).reshape(LANES, 1)
    sgn_i = jnp.asarray(np.where(lane % IDX_HEAD_DIM < half_i, -1.0, 1.0).astype(np.float32)).reshape(LANES, 1)
    col = pl.BlockSpec((LANES, 1), lambda b, i: (0, 0))
    tab = pl.BlockSpec((1, ts, LANES), lambda b, i: (b, i, 0))
    shp = jax.ShapeDtypeStruct((B, S, LANES), F32)
    return pl.pallas_call(
        _rope_body,
        grid=(B, S // ts),
        in_specs=[pl.BlockSpec((1, 1, ts), lambda b, i: (b, 0, i)), col, col, col, col],
        out_specs=[tab, tab, tab, tab],
        out_shape=[shp, shp, shp, shp],
        compiler_params=_cparams(2),
        name="rope_tables",
    )(positions.reshape(B, 1, S), inv_a, sgn_a, inv_i, sgn_i)


def _rope_a(x, cos, sin_signed):
    return x * cos + pltpu.roll(x, ATTN_HEAD_DIM // 2, 1) * sin_signed


def _rope_i(x, cos, sin_signed, lo_half):
    half = IDX_HEAD_DIM // 2
    partner = jnp.where(lo_half, pltpu.roll(x, LANES - half, 1), pltpu.roll(x, half, 1))
    return x * cos + partner * sin_signed


def _inproj_body(x_ref, sh_ref, sc_ref, w_ref, ca_ref, sa_ref, ci_ref, si_ref,
                 qT_ref, k_ref, vT_ref, qiT_ref, kia_ref, kib_ref, wiT_ref, xr_ref, yg_ref):
    tm = x_ref.shape[1]
    n_chunk = tm // KEY_CHUNK
    h = (x_ref[0] * (1.0 + sc_ref[0]) + sh_ref[0]).astype(BF16)
    ca, sa, ci, si = ca_ref[0], sa_ref[0], ci_ref[0], si_ref[0]
    lane = lax.broadcasted_iota(jnp.int32, (tm, LANES), 1)
    lo_half = (lane % IDX_HEAD_DIM) < (IDX_HEAD_DIM // 2)
    q_scale = ATTN_HEAD_DIM ** -0.5 * math.log2(math.e)

    def seg(start, width=LANES):
        return jnp.dot(h, w_ref[:, start:start + width], preferred_element_type=F32)

    def store_T(ref, g, val):
        vt = val.T.astype(BF16)
        for c in range(n_chunk):
            ref[0, c, g * LANES:(g + 1) * LANES, :] = vt[:, c * KEY_CHUNK:(c + 1) * KEY_CHUNK]

    def lane_groups(start):
        for pair in range(ATTN_WIDTH // MXU_COLS):
            wide = seg(start + pair * MXU_COLS, MXU_COLS)
            for half in range(MXU_COLS // LANES):
                yield pair * (MXU_COLS // LANES) + half, wide[:, half * LANES:(half + 1) * LANES]

    for g, val in lane_groups(SEG_Q):
        store_T(qT_ref, g, _rope_a(val, ca, sa) * q_scale)
    for g, val in lane_groups(SEG_K):
        k_ref[0, :, g * LANES:(g + 1) * LANES] = _rope_a(val, ca, sa).astype(BF16)
    for g, val in lane_groups(SEG_V):
        store_T(vT_ref, g, val)
    for g, val in lane_groups(SEG_QI):
        store_T(qiT_ref, g, _rope_i(val, ci, si, lo_half))
    xr_ref[0] = seg(SEG_XR, LRU_WIDTH)
    yg_ref[0] = seg(SEG_YG, LRU_WIDTH)
    kiw = seg(SEG_KIW)
    ki = jnp.where(lane < IDX_HEAD_DIM, _rope_i(kiw, ci, si, lo_half), 0.0)
    kia_ref[0] = ki.astype(BF16)
    kib_ref[0] = pltpu.roll(ki, IDX_HEAD_DIM, 1).astype(BF16)
    wiT = kiw.T[IDX_HEAD_DIM:IDX_HEAD_DIM + IDX_HEADS, :] * (IDX_HEADS ** -0.5)
    for c in range(n_chunk):
        wiT_ref[0, c] = wiT[:, c * KEY_CHUNK:(c + 1) * KEY_CHUNK]


def _inproj_call(x, mod3, w_in_p, tabs):
    B, S, D = x.shape
    tm = min(ROW_TILE, S)
    nc = S // KEY_CHUNK
    cpb = tm // KEY_CHUNK
    ca, sa, ci, si = tabs
    tab = pl.BlockSpec((1, tm, LANES), lambda b, i: (b, i, 0))
    tspec = pl.BlockSpec((1, cpb, ATTN_WIDTH, KEY_CHUNK), lambda b, i: (b, i, 0, 0))
    tshape = jax.ShapeDtypeStruct((B, nc, ATTN_WIDTH, KEY_CHUNK), BF16)
    rows = lambda w, dt: (pl.BlockSpec((1, tm, w), lambda b, i: (b, i, 0)), jax.ShapeDtypeStruct((B, S, w), dt))
    k_spec, k_shape = rows(ATTN_WIDTH, BF16)
    ki_spec, ki_shape = rows(LANES, BF16)
    r_spec, r_shape = rows(LRU_WIDTH, F32)
    return pl.pallas_call(
        _inproj_body,
        grid=(B, S // tm),
        in_specs=[pl.BlockSpec((1, tm, D), lambda b, i: (b, i, 0)),
                  pl.BlockSpec((1, 1, D), lambda b, i: (b, 0, 0)),
                  pl.BlockSpec((1, 1, D), lambda b, i: (b, 0, 1)),
                  pl.BlockSpec((D, IN_COLS_PAD), lambda b, i: (0, 0), pipeline_mode=pl.Buffered(1)),
                  tab, tab, tab, tab],
        out_specs=[tspec, k_spec, tspec, tspec, ki_spec, ki_spec,
                   pl.BlockSpec((1, cpb, IDX_HEADS, KEY_CHUNK), lambda b, i: (b, i, 0, 0)),
                   r_spec, r_spec],
        out_shape=[tshape, k_shape, tshape, tshape, ki_shape, ki_shape,
                   jax.ShapeDtypeStruct((B, nc, IDX_HEADS, KEY_CHUNK), F32),
                   r_shape, r_shape],
        compiler_params=_cparams(2),
        name="in_proj",
    )(x, mod3, mod3, w_in_p, ca, sa, ci, si)


_BUTTERFLY_MASK = {16: 0x0000FFFF, 8: 0x00FF00FF, 4: 0x0F0F0F0F, 2: 0x33333333, 1: 0x55555555}


def _butterfly(lo, hi, dist):
    t = (lo ^ lax.shift_right_logical(hi, jnp.int32(dist))) & jnp.int32(_BUTTERFLY_MASK[dist])
    return lo ^ t, hi ^ jnp.left_shift(t, jnp.int32(dist))


def _bit_transpose32(load, store):
    for k in range(8):
        a, b, c, d = load(k), load(k + 8), load(k + 16), load(k + 24)
        a, c = _butterfly(a, c, 16)
        b, d = _butterfly(b, d, 16)
        a, b = _butterfly(a, b, 8)
        c, d = _butterfly(c, d, 8)
        store(k, a), store(k + 8, b), store(k + 16, c), store(k + 24, d)
    for g in range(0, 32, 8):
        w = [store(g + x) for x in range(8)]
        for dist in (4, 2, 1):
            for x in range(8):
                if not x & dist:
                    w[x], w[x + dist] = _butterfly(w[x], w[x + dist], dist)
        for x in range(8):
            store(g + x, w[x])


def _fold_rows(x, op):
    r, c = x.shape
    x = x.reshape(r // SUBLANES, SUBLANES, c)
    return op(x, axis=0)


def _dsa_body(qT_ref, qiT_ref, wiT_ref, k_ref, vT_ref, kia_ref, kib_ref, g_ref, o_ref,
              key_s, plane_s, alive_s, bias_s, sc0_s, sc1_s, s0_s, s1_s, p_s, m_s, l_s, acc_s,
              *, topk, idx_bits, group):
    sc_s = (sc0_s, sc1_s)
    s_s = (s0_s, s1_s)
    i = pl.program_id(1)
    tq = KEY_CHUNK
    tk = KEY_CHUNK
    n_chunks = i + 1
    q_pos = i * tq + lax.broadcasted_iota(jnp.int32, (1, tq), 1)

    slabs = [slice(r, r + ATTN_SLAB) for r in range(0, tk, ATTN_SLAB)]

    def pipelined(produce, consume):
        produce(0, 0)

        def pair(t, carry):
            c = 2 * t
            produce(c + 1, 1)
            consume(c, 0, False)
            produce(c + 2, 0)
            consume(c + 1, 1, False)
            return carry
        lax.fori_loop(0, i // 2, pair, 0)

        @pl.when(i % 2 == 0)
        def _():
            consume(i, 0, True)

        @pl.when(i % 2 == 1)
        def _():
            produce(i, 1)
            consume(i - 1, 0, False)
            consume(i, 1, True)

    def score_matmuls(c, buf):
        r0 = pl.multiple_of(c * tk, tk)
        ka = kia_ref[0, pl.ds(r0, tk), :]
        kb = kib_ref[0, pl.ds(r0, tk), :]
        for p in range(IDX_HEADS // 2):
            qp = qiT_ref[0, 0, p * LANES:(p + 1) * LANES, :]
            sc_s[buf][2 * p, 0:tk, :] = jnp.dot(ka, qp, preferred_element_type=F32)
            sc_s[buf][2 * p + 1, 0:tk, :] = jnp.dot(kb, qp, preferred_element_type=F32)

    def score_keys(c, buf, diagonal):
        c0 = pl.multiple_of(c * tk, tk)
        for sl in slabs:
            acc = wiT_ref[0, 0, 0:1, :] * jnp.maximum(sc_s[buf][0, sl, :], 0.0)
            for hd in range(1, IDX_HEADS):
                acc = acc + wiT_ref[0, 0, hd:hd + 1, :] * jnp.maximum(sc_s[buf][hd, sl, :], 0.0)
            acc = acc + 0.0
            bits = pltpu.bitcast(acc, jnp.int32)
            mono = bits ^ ((bits >> 31) & 0x7FFFFFFF)
            if diagonal:
                k_pos = c0 + sl.start + lax.broadcasted_iota(jnp.int32, (ATTN_SLAB, 1), 0)
                mono = jnp.where(k_pos <= q_pos, mono, KEY_NEG_INF)
            key_s[pl.ds(c0 + sl.start, ATTN_SLAB), :] = mono
        for lh in range(tq // LANES):
            ls = slice(lh * LANES, (lh + 1) * LANES)
            def key_word(r, ls=ls):
                return key_s[pl.ds(c0 + SUBLANES * r, SUBLANES), ls]

            def plane_word(b, val=None, ls=ls):
                if val is None:
                    return plane_s[c, b, :, ls]
                plane_s[c, b, :, ls] = val

            _bit_transpose32(key_word, plane_word)
            plane_s[c, 0, :, ls] = ~plane_s[c, 0, :, ls]

    pipelined(score_matmuls, score_keys)

    n_groups = (n_chunks + group - 1) // group

    def pad_chunk(c, carry):
        plane_s[c] = jnp.zeros((32, SUBLANES, tq), jnp.int32)
        alive_s[c] = jnp.zeros((SUBLANES, tq), jnp.int32)
        return carry
    lax.fori_loop(n_chunks, n_groups * group, pad_chunk, 0)

    def sweep(j, keep_mask):
        def grp(g, part):
            for cc in range(group):
                c = g * group + cc
                alive = alive_s[c]
                if keep_mask is not None:
                    alive = alive & (plane_s[c, j - 1] ^ keep_mask)
                    alive_s[c] = alive
                part = part + lax.population_count(alive & plane_s[c, j])
            return part
        part = lax.fori_loop(0, n_groups, grp, jnp.zeros((SUBLANES, tq), jnp.int32))
        return part.sum(axis=0, keepdims=True)

    def decide(j, c1, state):
        thr_u, n_gt, n_alive = state
        ok = n_gt + c1 >= topk
        thr_u = thr_u | jnp.where(ok, jnp.left_shift(jnp.int32(1), 31 - j), 0)
        return jnp.where(ok, 0, -1), (thr_u, jnp.where(ok, n_gt, n_gt + c1), jnp.where(ok, c1, n_alive - c1))

    def init_alive(c, carry):
        alive_s[c] = jnp.full((SUBLANES, tq), -1, jnp.int32)
        return carry
    lax.fori_loop(0, n_chunks, init_alive, 0)
    zero = jnp.zeros((1, tq), jnp.int32)
    flip, state = decide(0, sweep(0, None), (zero, zero, zero + n_chunks * tk))

    def bit_step(j, carry):
        flip_prev, state = carry
        return decide(j, sweep(j, jnp.broadcast_to(flip_prev, (SUBLANES, tq))), state)

    flip_last, (thr_u, n_gt, n_alive) = lax.fori_loop(1, 32, bit_step, (flip, state))
    thr = thr_u ^ INT_MIN
    cnt_thr = n_gt + n_alive

    @pl.when(jnp.max(cnt_thr) > topk)
    def _():
        need = topk - n_gt
        def row_idx(c):
            return c * tk + lax.broadcasted_iota(jnp.int32, (tk, 1), 0)
        flip8 = jnp.broadcast_to(flip_last, (SUBLANES, tq))
        def settle(c, carry):
            alive_s[c] = alive_s[c] & (plane_s[c, 31] ^ flip8)
            return carry
        lax.fori_loop(0, n_chunks, settle, 0)
        def count_tied_below(cand):
            def grp(g, part):
                for cc in range(group):
                    c = g * group + cc
                    first_row = c * tk + lax.broadcasted_iota(jnp.int32, (SUBLANES, 1), 0)
                    n_below = jnp.clip((cand - first_row + (SUBLANES - 1)) >> 3, 0, 32)
                    top_bits = jnp.where(n_below > 0,
                                         jnp.left_shift(jnp.int32(-1), 32 - jnp.maximum(n_below, 1)), 0)
                    part = part + lax.population_count(alive_s[c] & top_bits)
                return part
            part = lax.fori_loop(0, n_groups, grp, jnp.zeros((SUBLANES, tq), jnp.int32))
            return part.sum(axis=0, keepdims=True)
        def idx_step(j, cut):
            cand = cut | jnp.left_shift(jnp.int32(1), idx_bits - 1 - j)
            return jnp.where(count_tied_below(cand) < need, cand, cut)
        cut = lax.fori_loop(0, idx_bits, idx_step, jnp.zeros((1, tq), jnp.int32))
        fix = cnt_thr > topk
        def rewrite(c, carry):
            sl = pl.ds(pl.multiple_of(c * tk, tk), tk)
            blk = key_s[sl, :]
            drop = (blk == thr) & (row_idx(c) > cut) & fix
            key_s[sl, :] = jnp.where(drop, blk - 1, blk)
            return carry
        lax.fori_loop(0, n_chunks, rewrite, 0)

    m_s[...] = jnp.full(m_s.shape, -jnp.inf, F32)
    l_s[...] = jnp.zeros(l_s.shape, F32)
    acc_s[...] = jnp.zeros(acc_s.shape, F32)

    heads = [slice(hd * ATTN_HEAD_DIM, (hd + 1) * ATTN_HEAD_DIM) for hd in range(ATTN_HEADS)]

    def logit_matmuls(c, buf):
        r0 = pl.multiple_of(c * tk, tk)
        for hd, hs in enumerate(heads):
            s_s[buf][hd, 0:tk, :] = jnp.dot(k_ref[0, pl.ds(r0, tk), hs], qT_ref[0, 0, hs, :],
                                             preferred_element_type=F32)

    def attn_chunk(c, buf, diagonal):
        r0 = pl.multiple_of(c * tk, tk)
        logits = s_s[buf]
        for sl in slabs:
            sel = key_s[pl.ds(r0 + sl.start, ATTN_SLAB), :] >= thr
            if diagonal:
                k_pos = r0 + sl.start + lax.broadcasted_iota(jnp.int32, (ATTN_SLAB, 1), 0)
                sel = sel & (k_pos <= q_pos)
            bias_s[sl, :] = jnp.where(sel, 0.0, -jnp.inf)
        m_safe, alpha = [], []
        for hd in range(ATTN_HEADS):
            m_old = m_s[hd]
            mx = None
            for sl in slabs:
                masked = logits[hd, sl, :] + bias_s[sl, :]
                logits[hd, sl, :] = masked
                fold = _fold_rows(masked, jnp.max)
                mx = fold if mx is None else jnp.maximum(mx, fold)
            m_new = jnp.maximum(m_old, mx.max(axis=0, keepdims=True))
            m_s[hd] = m_new
            m_safe.append(jnp.where(m_new == -jnp.inf, 0.0, m_new))
            alpha.append(jnp.exp2(m_old - m_safe[hd]))
        for hd in range(ATTN_HEADS):
            psum = jnp.zeros((SUBLANES, tq), F32)
            for sl in slabs:
                p = jnp.exp2(logits[hd, sl, :] - m_safe[hd])
                psum = psum + _fold_rows(p, jnp.sum)
                p_s[hd, sl, :] = p.astype(BF16)
            l_s[hd] = alpha[hd] * l_s[hd] + psum.sum(axis=0, keepdims=True)
        for hd, hs in enumerate(heads):
            acc_s[hd] = alpha[hd] * acc_s[hd] + jnp.dot(vT_ref[0, c, hs, :], p_s[hd],
                                                        preferred_element_type=F32)

    pipelined(logit_matmuls, attn_chunk)

    outT = jnp.concatenate([acc_s[hd] / l_s[hd] for hd in range(ATTN_HEADS)], axis=0)
    out = outT.T
    ms = jnp.mean(jnp.square(out), axis=-1, keepdims=True)
    o_ref[0] = (out * lax.rsqrt(ms + LN_EPS) * g_ref[...]).astype(BF16)


def _dsa_call(qT, qiT, wiT, k, vT, kia, kib, g_attn):
    B, S, _ = k.shape
    nc = S // KEY_CHUNK
    topk = min(TOPK_MAX, S // 4)
    idx_bits = max(1, int(math.ceil(math.log2(S))))
    group = math.gcd(SWEEP_GROUP, nc)
    blk_q = pl.BlockSpec((1, 1, ATTN_WIDTH, KEY_CHUNK), lambda b, i: (b, i, 0, 0))
    full = lambda shape: pl.BlockSpec(shape, lambda b, i: (b,) + (0,) * (len(shape) - 1))
    return pl.pallas_call(
        functools.partial(_dsa_body, topk=topk, idx_bits=idx_bits, group=group),
        grid=(B, nc),
        in_specs=[blk_q, blk_q,
                  pl.BlockSpec((1, 1, IDX_HEADS, KEY_CHUNK), lambda b, i: (b, i, 0, 0)),
                  full((1, S, ATTN_WIDTH)),
                  full((1, nc, ATTN_WIDTH, KEY_CHUNK)),
                  full((1, S, LANES)), full((1, S, LANES)),
                  pl.BlockSpec((1, ATTN_WIDTH), lambda b, i: (0, 0))],
        out_specs=pl.BlockSpec((1, KEY_CHUNK, ATTN_WIDTH), lambda b, i: (b, i, 0)),
        out_shape=jax.ShapeDtypeStruct((B, S, ATTN_WIDTH), BF16),
        scratch_shapes=[pltpu.VMEM((S, KEY_CHUNK), jnp.int32),
                        pltpu.VMEM((nc, 32, SUBLANES, KEY_CHUNK), jnp.int32),
                        pltpu.VMEM((nc, SUBLANES, KEY_CHUNK), jnp.int32),
                        pltpu.VMEM((KEY_CHUNK, KEY_CHUNK), F32),
                        pltpu.VMEM((IDX_HEADS, HEAD_PITCH, KEY_CHUNK), F32),
                        pltpu.VMEM((IDX_HEADS, HEAD_PITCH, KEY_CHUNK), F32),
                        pltpu.VMEM((ATTN_HEADS, HEAD_PITCH, KEY_CHUNK), F32),
                        pltpu.VMEM((ATTN_HEADS, HEAD_PITCH, KEY_CHUNK), F32),
                        pltpu.VMEM((ATTN_HEADS, KEY_CHUNK, KEY_CHUNK), BF16),
                        pltpu.VMEM((ATTN_HEADS, 1, KEY_CHUNK), F32),
                        pltpu.VMEM((ATTN_HEADS, 1, KEY_CHUNK), F32),
                        pltpu.VMEM((ATTN_HEADS, ATTN_HEAD_DIM, KEY_CHUNK), F32)],
        compiler_params=_cparams(2),
        name="dsa_attention",
    )(qT, qiT, wiT, k, vT, kia, kib, g_attn.reshape(1, ATTN_WIDTH))


def _lru_body(xr_ref, yg_ref, cw_ref, cb_ref, wa_ref, wx_ref, ba_ref, bx_ref, lam_ref, g_ref, o_ref,
              xbuf_s, a_s, u_s, h_s, carry_s):
    ts = xr_ref.shape[1]
    W = xr_ref.shape[2]
    pad = SUBLANES

    @pl.when(pl.program_id(1) == 0)
    def _():
        xbuf_s[0:pad, :] = jnp.zeros((pad, W), F32)
        carry_s[...] = jnp.zeros(carry_s.shape, F32)

    xbuf_s[pad:pad + ts, :] = xr_ref[0]
    xc = cb_ref[...]
    for j in range(CONV_WIDTH):
        xc = xc + xbuf_s[pad - (CONV_WIDTH - 1) + j:pad - (CONV_WIDTH - 1) + j + ts, :] * cw_ref[j:j + 1, :]
    xbuf_s[0:pad, :] = xbuf_s[ts:ts + pad, :]

    xcb = xc.astype(BF16)
    r = jax.nn.sigmoid(jnp.dot(xcb, wa_ref[...], preferred_element_type=F32) + ba_ref[...])
    gi = jax.nn.sigmoid(jnp.dot(xcb, wx_ref[...], preferred_element_type=F32) + bx_ref[...])
    neg_lam = -lam_ref[...]
    softplus = jnp.maximum(neg_lam, 0.0) + jnp.log1p(jnp.exp(-jnp.abs(neg_lam)))
    log_a = -LRU_C * r * softplus
    a = jnp.exp(log_a)
    u = jnp.sqrt(-jnp.tanh(log_a) * (a * a + 1.0)) * (gi * xc)

    row = lax.broadcasted_iota(jnp.int32, (ts, 1), 0) % SUBLANES
    d = 1
    while d < SUBLANES:
        keep = row >= d
        a_sh = jnp.where(keep, pltpu.roll(a, d, 0), 1.0)
        u_sh = jnp.where(keep, pltpu.roll(u, d, 0), 0.0)
        u = a * u_sh + u
        a = a * a_sh
        d *= 2
    a_s[...] = a
    u_s[...] = u

    def group(g, hprev):
        sl = pl.ds(pl.multiple_of(g * SUBLANES, SUBLANES), SUBLANES)
        hg = a_s[sl, :] * hprev + u_s[sl, :]
        h_s[sl, :] = hg
        return hg[SUBLANES - 1:SUBLANES, :]

    carry_s[...] = lax.fori_loop(0, ts // SUBLANES, group, carry_s[...], unroll=8)

    y = h_s[...] * jax.nn.gelu(yg_ref[0], approximate=True)
    ms = jnp.mean(jnp.square(y), axis=-1, keepdims=True)
    o_ref[0] = (y * lax.rsqrt(ms + LN_EPS) * g_ref[...]).astype(BF16)


def _lru_call(xr, yg, conv_w, conv_b, wa_bd, wx_bd, b_a, b_x, lam, g_lru):
    B, S, W = xr.shape
    ts = min(ROW_TILE, S)
    rows = pl.BlockSpec((1, ts, W), lambda b, i: (b, i, 0))
    vec = pl.BlockSpec((1, W), lambda b, i: (0, 0))
    mat = pl.BlockSpec((W, W), lambda b, i: (0, 0))
    return pl.pallas_call(
        _lru_body,
        grid=(B, S // ts),
        in_specs=[rows, rows, pl.BlockSpec((CONV_WIDTH, W), lambda b, i: (0, 0)), vec, mat, mat,
                  vec, vec, vec, vec],
        out_specs=rows,
        out_shape=jax.ShapeDtypeStruct((B, S, W), BF16),
        scratch_shapes=[pltpu.VMEM((ts + SUBLANES, W), F32),
                        pltpu.VMEM((ts, W), F32), pltpu.VMEM((ts, W), F32), pltpu.VMEM((ts, W), F32),
                        pltpu.VMEM((1, W), F32)],
        compiler_params=_cparams(2),
        name="rg_lru",
    )(xr, yg, conv_w, conv_b.reshape(1, W), wa_bd, wx_bd, b_a.reshape(1, W), b_x.reshape(1, W),
      lam.reshape(1, W), g_lru.reshape(1, W))


def _layer_norm(y, g, b):
    mu = jnp.mean(y, axis=-1, keepdims=True)
    yc = y - mu
    var = jnp.mean(jnp.square(yc), axis=-1, keepdims=True)
    return yc * lax.rsqrt(var + LN_EPS) * g + b


def _mlp_body(at_ref, lr_ref, x_ref, g1_ref, sh2_ref, sc2_ref, g2_ref, wo_ref, l1g_ref, l1b_ref,
              wu_ref, wd_ref, l2g_ref, l2b_ref, o_ref, *, alpha, ff_chunk):
    aw = at_ref.shape[2]
    mix = jnp.dot(at_ref[0], wo_ref[0:aw, :], preferred_element_type=F32)
    mix = mix + jnp.dot(lr_ref[0], wo_ref[aw:, :], preferred_element_type=F32)
    x1 = _layer_norm(alpha * x_ref[0] + (1.0 + g1_ref[0]) * mix, l1g_ref[...], l1b_ref[...])
    h2 = (x1 * (1.0 + sc2_ref[0]) + sh2_ref[0]).astype(BF16)
    d_ff = wu_ref.shape[1]
    ff = jnp.zeros(x1.shape, F32)
    for c in range(d_ff // ff_chunk):
        cs = slice(c * ff_chunk, (c + 1) * ff_chunk)
        up = jnp.dot(h2, wu_ref[:, cs], preferred_element_type=F32)
        act = jnp.square(jnp.maximum(up, 0.0)).astype(BF16)
        ff = ff + jnp.dot(act, wd_ref[cs, :], preferred_element_type=F32)
    o_ref[0] = _layer_norm(alpha * x1 + (1.0 + g2_ref[0]) * ff, l2g_ref[...], l2b_ref[...])


def _mlp_call(attn_n, lru_n, x, mod3, w_out, ln1_g, ln1_b, w_up, w_down, ln2_g, ln2_b, alpha):
    B, S, D = x.shape
    tm = min(ROW_TILE, S)
    d_ff = w_up.shape[1]
    half = pl.BlockSpec((1, tm, attn_n.shape[2]), lambda b, i: (b, i, 0))
    rows = pl.BlockSpec((1, tm, D), lambda b, i: (b, i, 0))
    modv = lambda j: pl.BlockSpec((1, 1, D), lambda b, i: (b, 0, j))
    vec = pl.BlockSpec((1, D), lambda b, i: (0, 0))
    wspec = lambda shape: pl.BlockSpec(shape, lambda b, i: (0, 0), pipeline_mode=pl.Buffered(1))
    return pl.pallas_call(
        functools.partial(_mlp_body, alpha=alpha, ff_chunk=min(1024, d_ff)),
        grid=(B, S // tm),
        in_specs=[half, half, rows, modv(2), modv(3), modv(4), modv(5),
                  wspec((D, D)), vec, vec, wspec((D, d_ff)), wspec((d_ff, D)), vec, vec],
        out_specs=rows,
        out_shape=jax.ShapeDtypeStruct((B, S, D), F32),
        compiler_params=_cparams(2),
        name="out_mlp",
    )(attn_n, lru_n, x, mod3, mod3, mod3, mod3, w_out, ln1_g.reshape(1, D), ln1_b.reshape(1, D),
      w_up, w_down, ln2_g.reshape(1, D), ln2_b.reshape(1, D))


def _prep_w_in(w):
    D = w.shape[0]
    o_ki = 3 * ATTN_WIDTH + IDX_WIDTH
    o_xr = o_ki + IDX_HEAD_DIM + IDX_HEADS
    pad = IN_COLS_PAD - (o_xr + 2 * LRU_WIDTH)
    return jnp.concatenate([w[:, :o_ki], w[:, o_xr:], w[:, o_ki:o_xr], jnp.zeros((D, pad), w.dtype)],
                           axis=1).astype(BF16)


def _block_diag(w):
    n, c, d = w.shape
    eye = jnp.eye(n, dtype=w.dtype)
    return (w[:, :, None, :] * eye[:, None, :, None]).reshape(n * c, n * d).astype(BF16)


def kernel(x, c, positions, w_mod, b_mod, w_in, conv_w, conv_b, w_gate_a, b_gate_a, w_gate_x, b_gate_x,
           lru_lambda, g_attn_out, g_lru_out, w_out, ln1_g, ln1_b, w_up, w_down, ln2_g, ln2_b):
    B, S, D = x.shape
    L = w_mod.shape[0]
    assert S % KEY_CHUNK == 0 and S % min(ROW_TILE, S) == 0 and D % LANES == 0
    assert w_in.shape[2] == 3 * ATTN_WIDTH + IDX_WIDTH + IDX_HEAD_DIM + IDX_HEADS + 2 * LRU_WIDTH
    alpha = (2.0 * L) ** 0.25

    mod = _mod_call(c, w_mod, b_mod)
    tabs = _rope_call(positions)
    for l in range(L):
        mod3 = mod[l].reshape(B, 1, N_MOD * D)
        qT, k, vT, qiT, kia, kib, wiT, xr, yg = _inproj_call(x, mod3, _prep_w_in(w_in[l]), tabs)
        attn_n = _dsa_call(qT, qiT, wiT, k, vT, kia, kib, g_attn_out[l])
        lru_n = _lru_call(xr, yg, conv_w[l], conv_b[l], _block_diag(w_gate_a[l]), _block_diag(w_gate_x[l]),
                          b_gate_a[l], b_gate_x[l], lru_lambda[l], g_lru_out[l])
        x = _mlp_call(attn_n, lru_n, x, mod3, w_out[l].astype(BF16), ln1_g[l], ln1_b[l],
                      w_up[l].astype(BF16), w_down[l].astype(BF16), ln2_g[l], ln2_b[l], alpha)
    return x
```

```python
import functools
import math

import jax
import jax.numpy as jnp
import numpy as np
from jax import lax
from jax.experimental import pallas as pl
from jax.experimental.pallas import tpu as pltpu

ATTN_HEAD_DIM = 128
ATTN_HEADS = 4
ATTN_WIDTH = ATTN_HEADS * ATTN_HEAD_DIM
IDX_HEADS = 8
IDX_HEAD_DIM = 64
IDX_WIDTH = IDX_HEADS * IDX_HEAD_DIM
TOPK_MAX = 256
LRU_WIDTH = 512
LRU_BLOCKS = 8
LRU_BLOCK = LRU_WIDTH // LRU_BLOCKS
CONV_WIDTH = 4
LRU_C = 8.0
ROPE_THETA = 10000.0
LN_EPS = 1e-5
N_MOD = 6

LANES = 128
SUBLANES = 8
MXU_COLS = 256
VMEM_LIMIT_BYTES = 56 * 1024 * 1024

KEY_CHUNK = 256
ROW_TILE = 512
IDX_SUB = 128
ATTN_SLAB = 64
SWEEP_GROUP = 4
HEAD_PITCH = KEY_CHUNK

SEG_Q, SEG_K, SEG_V, SEG_QI, SEG_XR, SEG_YG, SEG_KIW = 0, 512, 1024, 1536, 2048, 2560, 3072
IN_COLS_PAD = 3200

BF16 = jnp.bfloat16
F32 = jnp.float32
INT_MIN = -2 ** 31
KEY_NEG_INF = int(np.int32(np.uint32(0xFF800000) ^ np.uint32(0x7FFFFFFF)))


def _cparams(n_grid):
    return pltpu.CompilerParams(
        dimension_semantics=("arbitrary",) * n_grid,
        vmem_limit_bytes=VMEM_LIMIT_BYTES)


def _mod_body(c_ref, w_ref, b_ref, o_ref):
    c = c_ref[...]
    ca = (c * jax.nn.sigmoid(c)).astype(BF16)
    o_ref[0] = jnp.dot(ca, w_ref[0].astype(BF16), preferred_element_type=F32) + b_ref[0]


def _mod_call(c, w_mod, b_mod):
    L, D, N = w_mod.shape
    B = c.shape[0]
    tn = 2048 if N % 2048 == 0 else N
    return pl.pallas_call(
        _mod_body,
        grid=(L, N // tn),
        in_specs=[pl.BlockSpec((B, D), lambda l, j: (0, 0)),
                  pl.BlockSpec((1, D, tn), lambda l, j: (l, 0, j)),
                  pl.BlockSpec((1, 1, tn), lambda l, j: (l, 0, j))],
        out_specs=pl.BlockSpec((1, B, tn), lambda l, j: (l, 0, j)),
        out_shape=jax.ShapeDtypeStruct((L, B, N), F32),
        compiler_params=_cparams(2),
        name="adaln_mod",
    )(c, w_mod, b_mod.reshape(L, 1, N))


def _rope_body(pos_ref, inv_a_ref, sgn_a_ref, inv_i_ref, sgn_i_ref, ca_ref, sa_ref, ci_ref, si_ref):
    pos = pos_ref[0].astype(F32)
    ang_a = inv_a_ref[...] * pos
    ang_i = inv_i_ref[...] * pos
    ca_ref[0] = jnp.cos(ang_a).T
    sa_ref[0] = (jnp.sin(ang_a) * sgn_a_ref[...]).T
    ci_ref[0] = jnp.cos(ang_i).T
    si_ref[0] = (jnp.sin(ang_i) * sgn_i_ref[...]).T


def _rope_call(positions):
    B, S = positions.shape
    ts = min(ROW_TILE, S)
    half_a, half_i = ATTN_HEAD_DIM // 2, IDX_HEAD_DIM // 2
    inv_a = ROPE_THETA ** (-jnp.arange(0, ATTN_HEAD_DIM, 2, dtype=F32) / ATTN_HEAD_DIM)
    inv_i = ROPE_THETA ** (-jnp.arange(0, IDX_HEAD_DIM, 2, dtype=F32) / IDX_HEAD_DIM)
    inv_a = jnp.tile(inv_a, LANES // half_a).reshape(LANES, 1)
    inv_i = jnp.tile(inv_i, LANES // half_i).reshape(LANES, 1)
    lane = np.arange(LANES)
    sgn_a = jnp.asarray(np.where(lane % ATTN_HEAD_DIM < half_a, -1.0, 1.0).astype(np.float32)).reshape(LANES, 1)
    sgn_i = jnp.asarray(np.where(lane % IDX_HEAD_DIM < half_i, -1.0, 1.0).astype(np.float32)).reshape(LANES, 1)
    col = pl.BlockSpec((LANES, 1), lambda b, i: (0, 0))
    tab = pl.BlockSpec((1, ts, LANES), lambda b, i: (b, i, 0))
    shp = jax.ShapeDtypeStruct((B, S, LANES), F32)
    return pl.pallas_call(
        _rope_body,
        grid=(B, S // ts),
        in_specs=[pl.BlockSpec((1, 1, ts), lambda b, i: (b, 0, i)), col, col, col, col],
        out_specs=[tab, tab, tab, tab],
        out_shape=[shp, shp, shp, shp],
        compiler_params=_cparams(2),
        name="rope_tables",
    )(positions.reshape(B, 1, S), inv_a, sgn_a, inv_i, sgn_i)


def _rope_a(x, cos, sin_signed):
    return x * cos + pltpu.roll(x, ATTN_HEAD_DIM // 2, 1) * sin_signed


def _rope_i(x, cos, sin_signed, lo_half):
    half = IDX_HEAD_DIM // 2
    partner = jnp.where(lo_half, pltpu.roll(x, LANES - half, 1), pltpu.roll(x, half, 1))
    return x * cos + partner * sin_signed


def _inproj_body(x_ref, sh_ref, sc_ref, w_ref, ca_ref, sa_ref, ci_ref, si_ref,
                 qT_ref, k_ref, vT_ref, qiT_ref, kia_ref, kib_ref, wiT_ref, xr_ref, yg_ref):
    tm = x_ref.shape[1]
    n_chunk = tm // KEY_CHUNK
    h = (x_ref[0] * (1.0 + sc_ref[0]) + sh_ref[0]).astype(BF16)
    ca, sa, ci, si = ca_ref[0], sa_ref[0], ci_ref[0], si_ref[0]
    lane = lax.broadcasted_iota(jnp.int32, (tm, LANES), 1)
    lo_half = (lane % IDX_HEAD_DIM) < (IDX_HEAD_DIM // 2)
    q_scale = ATTN_HEAD_DIM ** -0.5 * math.log2(math.e)

    def seg(start, width=LANES):
        return jnp.dot(h, w_ref[:, start:start + width], preferred_element_type=F32)

    def store_T(ref, g, val):
        vt = val.T.astype(BF16)
        for c in range(n_chunk):
            ref[0, c, g * LANES:(g + 1) * LANES, :] = vt[:, c * KEY_CHUNK:(c + 1) * KEY_CHUNK]

    def lane_groups(start):
        for pair in range(ATTN_WIDTH // MXU_COLS):
            wide = seg(start + pair * MXU_COLS, MXU_COLS)
            for half in range(MXU_COLS // LANES):
                yield pair * (MXU_COLS // LANES) + half, wide[:, half * LANES:(half + 1) * LANES]

    for g, val in lane_groups(SEG_Q):
        store_T(qT_ref, g, _rope_a(val, ca, sa) * q_scale)
    for g, val in lane_groups(SEG_K):
        k_ref[0, :, g * LANES:(g + 1) * LANES] = _rope_a(val, ca, sa).astype(BF16)
    for g, val in lane_groups(SEG_V):
        store_T(vT_ref, g, val)
    for g, val in lane_groups(SEG_QI):
        store_T(qiT_ref, g, _rope_i(val, ci, si, lo_half))
    xr_ref[0] = seg(SEG_XR, LRU_WIDTH)
    yg_ref[0] = seg(SEG_YG, LRU_WIDTH)
    kiw = seg(SEG_KIW)
    ki = jnp.where(lane < IDX_HEAD_DIM, _rope_i(kiw, ci, si, lo_half), 0.0)
    kia_ref[0] = ki.astype(BF16)
    kib_ref[0] = pltpu.roll(ki, IDX_HEAD_DIM, 1).astype(BF16)
    wiT = kiw.T[IDX_HEAD_DIM:IDX_HEAD_DIM + IDX_HEADS, :] * (IDX_HEADS ** -0.5)
    for c in range(n_chunk):
        wiT_ref[0, c] = wiT[:, c * KEY_CHUNK:(c + 1) * KEY_CHUNK]


def _inproj_call(x, mod3, w_in_p, tabs):
    B, S, D = x.shape
    tm = min(ROW_TILE, S)
    nc = S // KEY_CHUNK
    cpb = tm // KEY_CHUNK
    ca, sa, ci, si = tabs
    tab = pl.BlockSpec((1, tm, LANES), lambda b, i: (b, i, 0))
    tspec = pl.BlockSpec((1, cpb, ATTN_WIDTH, KEY_CHUNK), lambda b, i: (b, i, 0, 0))
    tshape = jax.ShapeDtypeStruct((B, nc, ATTN_WIDTH, KEY_CHUNK), BF16)
    rows = lambda w, dt: (pl.BlockSpec((1, tm, w), lambda b, i: (b, i, 0)), jax.ShapeDtypeStruct((B, S, w), dt))
    k_spec, k_shape = rows(ATTN_WIDTH, BF16)
    ki_spec, ki_shape = rows(LANES, BF16)
    r_spec, r_shape = rows(LRU_WIDTH, F32)
    return pl.pallas_call(
        _inproj_body,
        grid=(B, S // tm),
        in_specs=[pl.BlockSpec((1, tm, D), lambda b, i: (b, i, 0)),
                  pl.BlockSpec((1, 1, D), lambda b, i: (b, 0, 0)),
                  pl.BlockSpec((1, 1, D), lambda b, i: (b, 0, 1)),
                  pl.BlockSpec((D, IN_COLS_PAD), lambda b, i: (0, 0), pipeline_mode=pl.Buffered(1)),
                  tab, tab, tab, tab],
        out_specs=[tspec, k_spec, tspec, tspec, ki_spec, ki_spec,
                   pl.BlockSpec((1, cpb, IDX_HEADS, KEY_CHUNK), lambda b, i: (b, i, 0, 0)),
                   r_spec, r_spec],
        out_shape=[tshape, k_shape, tshape, tshape, ki_shape, ki_shape,
                   jax.ShapeDtypeStruct((B, nc, IDX_HEADS, KEY_CHUNK), F32),
                   r_shape, r_shape],
        compiler_params=_cparams(2),
        name="in_proj",
    )(x, mod3, mod3, w_in_p, ca, sa, ci, si)


_BUTTERFLY_MASK = {16: 0x0000FFFF, 8: 0x00FF00FF, 4: 0x0F0F0F0F, 2: 0x33333333, 1: 0x55555555}


def _butterfly(lo, hi, dist):
    t = (lo ^ lax.shift_right_logical(hi, jnp.int32(dist))) & jnp.int32(_BUTTERFLY_MASK[dist])
    return lo ^ t, hi ^ jnp.left_shift(t, jnp.int32(dist))


def _bit_transpose32(load, store):
    for k in range(8):
        a, b, c, d = load(k), load(k + 8), load(k + 16), load(k + 24)
        a, c = _butterfly(a, c, 16)
        b, d = _butterfly(b, d, 16)
        a, b = _butterfly(a, b, 8)
        c, d = _butterfly(c, d, 8)
        store(k, a), store(k + 8, b), store(k + 16, c), store(k + 24, d)
    for g in range(0, 32, 8):
        w = [store(g + x) for x in range(8)]
        for dist in (4, 2, 1):
            for x in range(8):
                if not x & dist:
                    w[x], w[x + dist] = _butterfly(w[x], w[x + dist], dist)
        for x in range(8):
            store(g + x, w[x])


def _fold_rows(x, op):
    r, c = x.shape
    x = x.reshape(r // SUBLANES, SUBLANES, c)
    return op(x, axis=0)


def _dsa_body(qT_ref, qiT_ref, wiT_ref, k_ref, vT_ref, kia_ref, kib_ref, g_ref, o_ref,
              key_s, plane_s, alive_s, bias_s, wb_s, sc0_s, sc1_s, s0_s, s1_s, p_s, m_s, l_s, acc_s,
              *, topk, idx_bits, group):
    sc_s = (sc0_s, sc1_s)
    s_s = (s0_s, s1_s)
    i = pl.program_id(1)
    tq = KEY_CHUNK
    tk = KEY_CHUNK
    n_chunks = i + 1
    q_pos = i * tq + lax.broadcasted_iota(jnp.int32, (1, tq), 1)

    slabs = [slice(r, r + ATTN_SLAB) for r in range(0, tk, ATTN_SLAB)]

    def pipelined(produce, consume):
        produce(0, 0)

        def pair(t, carry):
            c = 2 * t
            produce(c + 1, 1)
            consume(c, 0, False)
            produce(c + 2, 0)
            consume(c + 1, 1, False)
            return carry
        lax.fori_loop(0, i // 2, pair, 0)

        @pl.when(i % 2 == 0)
        def _():
            consume(i, 0, True)

        @pl.when(i % 2 == 1)
        def _():
            produce(i, 1)
            consume(i - 1, 0, False)
            consume(i, 1, True)

    def score_matmuls(c, buf):
        r0 = pl.multiple_of(c * tk, tk)
        ka = kia_ref[0, pl.ds(r0, tk), :]
        kb = kib_ref[0, pl.ds(r0, tk), :]
        for p in range(IDX_HEADS // 2):
            qp = qiT_ref[0, 0, p * LANES:(p + 1) * LANES, :]
            sc_s[buf][2 * p, 0:tk, :] = jnp.dot(ka, qp, preferred_element_type=F32)
            sc_s[buf][2 * p + 1, 0:tk, :] = jnp.dot(kb, qp, preferred_element_type=F32)

    for hd in range(IDX_HEADS):
        wb_s[hd] = jnp.broadcast_to(wiT_ref[0, 0, hd:hd + 1, :], (SUBLANES, tq))

    def weighted_relu(buf, hd, sl):
        rel = jnp.maximum(sc_s[buf][hd, sl, :], 0.0).reshape(ATTN_SLAB // SUBLANES, SUBLANES, tq)
        return (rel * wb_s[hd]).reshape(ATTN_SLAB, tq)

    def score_keys(c, buf, diagonal):
        c0 = pl.multiple_of(c * tk, tk)
        for sl in slabs:
            acc = weighted_relu(buf, 0, sl)
            for hd in range(1, IDX_HEADS):
                acc = acc + weighted_relu(buf, hd, sl)
            bits = pltpu.bitcast(acc, jnp.int32)
            mono = bits ^ ((bits >> 31) & 0x7FFFFFFF)
            if diagonal:
                k_pos = c0 + sl.start + lax.broadcasted_iota(jnp.int32, (ATTN_SLAB, 1), 0)
                mono = jnp.where(k_pos <= q_pos, mono, KEY_NEG_INF)
            key_s[pl.ds(c0 + sl.start, ATTN_SLAB), :] = mono
        for lh in range(tq // LANES):
            ls = slice(lh * LANES, (lh + 1) * LANES)
            def key_word(r, ls=ls):
                return key_s[pl.ds(c0 + SUBLANES * r, SUBLANES), ls]

            def plane_word(b, val=None, ls=ls):
                if val is None:
                    return plane_s[c, b, :, ls]
                plane_s[c, b, :, ls] = val

            _bit_transpose32(key_word, plane_word)
            plane_s[c, 0, :, ls] = ~plane_s[c, 0, :, ls]

    pipelined(score_matmuls, score_keys)

    n_groups = (n_chunks + group - 1) // group

    def pad_chunk(c, carry):
        plane_s[c] = jnp.zeros((32, SUBLANES, tq), jnp.int32)
        alive_s[c] = jnp.zeros((SUBLANES, tq), jnp.int32)
        return carry
    lax.fori_loop(n_chunks, n_groups * group, pad_chunk, 0)

    def sweep(j, keep_mask):
        def grp(g, part):
            for cc in range(group):
                c = g * group + cc
                alive = alive_s[c]
                if keep_mask is not None:
                    alive = alive & (plane_s[c, j - 1] ^ keep_mask)
                    alive_s[c] = alive
                part = part + lax.population_count(alive & plane_s[c, j])
            return part
        part = lax.fori_loop(0, n_groups, grp, jnp.zeros((SUBLANES, tq), jnp.int32))
        return part.sum(axis=0, keepdims=True)

    def decide(j, c1, state):
        thr_u, n_gt, n_alive = state
        ok = n_gt + c1 >= topk
        thr_u = thr_u | jnp.where(ok, jnp.left_shift(jnp.int32(1), 31 - j), 0)
        return jnp.where(ok, 0, -1), (thr_u, jnp.where(ok, n_gt, n_gt + c1), jnp.where(ok, c1, n_alive - c1))

    def init_alive(c, carry):
        alive_s[c] = jnp.full((SUBLANES, tq), -1, jnp.int32)
        return carry
    lax.fori_loop(0, n_chunks, init_alive, 0)
    zero = jnp.zeros((1, tq), jnp.int32)
    flip, state = decide(0, sweep(0, None), (zero, zero, zero + n_chunks * tk))

    def bit_step(j, carry):
        flip_prev, state = carry
        return decide(j, sweep(j, jnp.broadcast_to(flip_prev, (SUBLANES, tq))), state)

    flip_last, (thr_u, n_gt, n_alive) = lax.fori_loop(1, 32, bit_step, (flip, state))
    thr = thr_u ^ INT_MIN
    cnt_thr = n_gt + n_alive

    @pl.when(jnp.max(cnt_thr) > topk)
    def _():
        need = topk - n_gt
        def row_idx(c):
            return c * tk + lax.broadcasted_iota(jnp.int32, (tk, 1), 0)
        flip8 = jnp.broadcast_to(flip_last, (SUBLANES, tq))
        def settle(c, carry):
            alive_s[c] = alive_s[c] & (plane_s[c, 31] ^ flip8)
            return carry
        lax.fori_loop(0, n_chunks, settle, 0)
        def count_tied_below(cand):
            def grp(g, part):
                for cc in range(group):
                    c = g * group + cc
                    first_row = c * tk + lax.broadcasted_iota(jnp.int32, (SUBLANES, 1), 0)
                    n_below = jnp.clip((cand - first_row + (SUBLANES - 1)) >> 3, 0, 32)
                    top_bits = jnp.where(n_below > 0,
                                         jnp.left_shift(jnp.int32(-1), 32 - jnp.maximum(n_below, 1)), 0)
                    part = part + lax.population_count(alive_s[c] & top_bits)
                return part
            part = lax.fori_loop(0, n_groups, grp, jnp.zeros((SUBLANES, tq), jnp.int32))
            return part.sum(axis=0, keepdims=True)
        def idx_step(j, cut):
            cand = cut | jnp.left_shift(jnp.int32(1), idx_bits - 1 - j)
            return jnp.where(count_tied_below(cand) < need, cand, cut)
        cut = lax.fori_loop(0, idx_bits, idx_step, jnp.zeros((1, tq), jnp.int32))
        fix = cnt_thr > topk
        def rewrite(c, carry):
            sl = pl.ds(pl.multiple_of(c * tk, tk), tk)
            blk = key_s[sl, :]
            drop = (blk == thr) & (row_idx(c) > cut) & fix
            key_s[sl, :] = jnp.where(drop, blk - 1, blk)
            return carry
        lax.fori_loop(0, n_chunks, rewrite, 0)

    m_s[...] = jnp.full(m_s.shape, -jnp.inf, F32)
    l_s[...] = jnp.zeros(l_s.shape, F32)
    acc_s[...] = jnp.zeros(acc_s.shape, F32)

    heads = [slice(hd * ATTN_HEAD_DIM, (hd + 1) * ATTN_HEAD_DIM) for hd in range(ATTN_HEADS)]

    def logit_matmuls(c, buf):
        r0 = pl.multiple_of(c * tk, tk)
        for hd, hs in enumerate(heads):
            s_s[buf][hd, 0:tk, :] = jnp.dot(k_ref[0, pl.ds(r0, tk), hs], qT_ref[0, 0, hs, :],
                                             preferred_element_type=F32)

    def attn_chunk(c, buf, diagonal):
        r0 = pl.multiple_of(c * tk, tk)
        logits = s_s[buf]
        for sl in slabs:
            sel = key_s[pl.ds(r0 + sl.start, ATTN_SLAB), :] >= thr
            if diagonal:
                k_pos = r0 + sl.start + lax.broadcasted_iota(jnp.int32, (ATTN_SLAB, 1), 0)
                sel = sel & (k_pos <= q_pos)
            bias_s[sl, :] = jnp.where(sel, 0.0, -jnp.inf)
        m_safe, alpha = [], []
        for hd in range(ATTN_HEADS):
            m_old = m_s[hd]
            mx = None
            for sl in slabs:
                masked = logits[hd, sl, :] + bias_s[sl, :]
                logits[hd, sl, :] = masked
                fold = _fold_rows(masked, jnp.max)
                mx = fold if mx is None else jnp.maximum(mx, fold)
            m_new = jnp.maximum(m_old, mx.max(axis=0, keepdims=True))
            m_s[hd] = m_new
            m_safe.append(jnp.where(m_new == -jnp.inf, 0.0, m_new))
            alpha.append(jnp.exp2(m_old - m_safe[hd]))
        for hd in range(ATTN_HEADS):
            psum = jnp.zeros((SUBLANES, tq), F32)
            for sl in slabs:
                p = jnp.exp2(logits[hd, sl, :] - m_safe[hd])
                psum = psum + _fold_rows(p, jnp.sum)
                p_s[hd, sl, :] = p.astype(BF16)
            l_s[hd] = alpha[hd] * l_s[hd] + psum.sum(axis=0, keepdims=True)
        for hd, hs in enumerate(heads):
            acc_s[hd] = alpha[hd] * acc_s[hd] + jnp.dot(vT_ref[0, c, hs, :], p_s[hd],
                                                        preferred_element_type=F32)

    pipelined(logit_matmuls, attn_chunk)

    outT = jnp.concatenate([acc_s[hd] / l_s[hd] for hd in range(ATTN_HEADS)], axis=0)
    out = outT.T
    ms = jnp.mean(jnp.square(out), axis=-1, keepdims=True)
    o_ref[0] = (out * lax.rsqrt(ms + LN_EPS) * g_ref[...]).astype(BF16)


def _dsa_call(qT, qiT, wiT, k, vT, kia, kib, g_attn):
    B, S, _ = k.shape
    nc = S // KEY_CHUNK
    topk = min(TOPK_MAX, S // 4)
    idx_bits = max(1, int(math.ceil(math.log2(S))))
    group = math.gcd(SWEEP_GROUP, nc)
    blk_q = pl.BlockSpec((1, 1, ATTN_WIDTH, KEY_CHUNK), lambda b, i: (b, i, 0, 0))
    full = lambda shape: pl.BlockSpec(shape, lambda b, i: (b,) + (0,) * (len(shape) - 1))
    return pl.pallas_call(
        functools.partial(_dsa_body, topk=topk, idx_bits=idx_bits, group=group),
        grid=(B, nc),
        in_specs=[blk_q, blk_q,
                  pl.BlockSpec((1, 1, IDX_HEADS, KEY_CHUNK), lambda b, i: (b, i, 0, 0)),
                  full((1, S, ATTN_WIDTH)),
                  full((1, nc, ATTN_WIDTH, KEY_CHUNK)),
                  full((1, S, LANES)), full((1, S, LANES)),
                  pl.BlockSpec((1, ATTN_WIDTH), lambda b, i: (0, 0))],
        out_specs=pl.BlockSpec((1, KEY_CHUNK, ATTN_WIDTH), lambda b, i: (b, i, 0)),
        out_shape=jax.ShapeDtypeStruct((B, S, ATTN_WIDTH), BF16),
        scratch_shapes=[pltpu.VMEM((S, KEY_CHUNK), jnp.int32),
                        pltpu.VMEM((nc, 32, SUBLANES, KEY_CHUNK), jnp.int32),
                        pltpu.VMEM((nc, SUBLANES, KEY_CHUNK), jnp.int32),
                        pltpu.VMEM((KEY_CHUNK, KEY_CHUNK), F32),
                        pltpu.VMEM((IDX_HEADS, SUBLANES, KEY_CHUNK), F32),
                        pltpu.VMEM((IDX_HEADS, HEAD_PITCH, KEY_CHUNK), F32),
                        pltpu.VMEM((IDX_HEADS, HEAD_PITCH, KEY_CHUNK), F32),
                        pltpu.VMEM((ATTN_HEADS, HEAD_PITCH, KEY_CHUNK), F32),
                        pltpu.VMEM((ATTN_HEADS, HEAD_PITCH, KEY_CHUNK), F32),
                        pltpu.VMEM((ATTN_HEADS, KEY_CHUNK, KEY_CHUNK), BF16),
                        pltpu.VMEM((ATTN_HEADS, 1, KEY_CHUNK), F32),
                        pltpu.VMEM((ATTN_HEADS, 1, KEY_CHUNK), F32),
                        pltpu.VMEM((ATTN_HEADS, ATTN_HEAD_DIM, KEY_CHUNK), F32)],
        compiler_params=_cparams(2),
        name="dsa_attention",
    )(qT, qiT, wiT, k, vT, kia, kib, g_attn.reshape(1, ATTN_WIDTH))


def _lru_body(xr_ref, yg_ref, cw_ref, cb_ref, wa_ref, wx_ref, ba_ref, bx_ref, lam_ref, g_ref, o_ref,
              xbuf_s, a_s, u_s, h_s, carry_s):
    ts = xr_ref.shape[1]
    W = xr_ref.shape[2]
    pad = SUBLANES

    @pl.when(pl.program_id(1) == 0)
    def _():
        xbuf_s[0:pad, :] = jnp.zeros((pad, W), F32)
        carry_s[...] = jnp.zeros(carry_s.shape, F32)

    xbuf_s[pad:pad + ts, :] = xr_ref[0]
    n_grp8 = ts // SUBLANES
    xg = xbuf_s[...].reshape(n_grp8 + 1, SUBLANES, W)
    sub = lax.broadcasted_iota(jnp.int32, (1, SUBLANES, 1), 1)
    xc = cb_ref[...] + xr_ref[0] * cw_ref[CONV_WIDTH - 1:CONV_WIDTH, :]
    for d in range(1, CONV_WIDTH):
        rot = pltpu.roll(xg, d, 1)
        tap = jnp.where(sub >= d, rot[1:], rot[:-1]).reshape(ts, W)
        xc = xc + tap * cw_ref[CONV_WIDTH - 1 - d:CONV_WIDTH - d, :]
    xbuf_s[0:pad, :] = xbuf_s[ts:ts + pad, :]

    xcb = xc.astype(BF16)
    r = jax.nn.sigmoid(jnp.dot(xcb, wa_ref[...], preferred_element_type=F32) + ba_ref[...])
    gi = jax.nn.sigmoid(jnp.dot(xcb, wx_ref[...], preferred_element_type=F32) + bx_ref[...])
    neg_lam = -lam_ref[...]
    softplus = jnp.maximum(neg_lam, 0.0) + jnp.log1p(jnp.exp(-jnp.abs(neg_lam)))
    log_a = -LRU_C * r * softplus
    a = jnp.exp(log_a)
    u = jnp.sqrt(-jnp.tanh(log_a) * (a * a + 1.0)) * (gi * xc)

    n_grp = ts // SUBLANES
    a = a.reshape(n_grp, SUBLANES, W)
    u = u.reshape(n_grp, SUBLANES, W)
    row = lax.broadcasted_iota(jnp.int32, (1, SUBLANES, 1), 1)
    d = 1
    while d < SUBLANES:
        keep = row >= d
        a_sh = jnp.where(keep, pltpu.roll(a, d, 1), 1.0)
        u_sh = jnp.where(keep, pltpu.roll(u, d, 1), 0.0)
        u = a * u_sh + u
        a = a * a_sh
        d *= 2
    a_s[...] = a.reshape(ts, W)
    u_s[...] = u.reshape(ts, W)

    last = SUBLANES - 1
    h_in = carry_s[last:SUBLANES, :]
    for g in range(n_grp):
        r = g * SUBLANES + last
        h_s[r:r + 1, :] = h_in
        h_in = a_s[r:r + 1, :] * h_in + u_s[r:r + 1, :]
    carry_s[last:SUBLANES, :] = h_in
    for g in range(n_grp):
        rows = slice(g * SUBLANES, (g + 1) * SUBLANES)
        r = g * SUBLANES + last
        h_s[rows, :] = a_s[rows, :] * h_s[r:r + 1, :] + u_s[rows, :]

    y = h_s[...] * jax.nn.gelu(yg_ref[0], approximate=True)
    ms = jnp.mean(jnp.square(y), axis=-1, keepdims=True)
    o_ref[0] = (y * lax.rsqrt(ms + LN_EPS) * g_ref[...]).astype(BF16)


def _lru_call(xr, yg, conv_w, conv_b, wa_bd, wx_bd, b_a, b_x, lam, g_lru):
    B, S, W = xr.shape
    ts = min(ROW_TILE, S)
    rows = pl.BlockSpec((1, ts, W), lambda b, i: (b, i, 0))
    vec = pl.BlockSpec((1, W), lambda b, i: (0, 0))
    mat = pl.BlockSpec((W, W), lambda b, i: (0, 0))
    return pl.pallas_call(
        _lru_body,
        grid=(B, S // ts),
        in_specs=[rows, rows, pl.BlockSpec((CONV_WIDTH, W), lambda b, i: (0, 0)), vec, mat, mat,
                  vec, vec, vec, vec],
        out_specs=rows,
        out_shape=jax.ShapeDtypeStruct((B, S, W), BF16),
        scratch_shapes=[pltpu.VMEM((ts + SUBLANES, W), F32),
                        pltpu.VMEM((ts, W), F32), pltpu.VMEM((ts, W), F32), pltpu.VMEM((ts, W), F32),
                        pltpu.VMEM((SUBLANES, W), F32)],
        compiler_params=_cparams(2),
        name="rg_lru",
    )(xr, yg, conv_w, conv_b.reshape(1, W), wa_bd, wx_bd, b_a.reshape(1, W), b_x.reshape(1, W),
      lam.reshape(1, W), g_lru.reshape(1, W))


def _layer_norm(y, g, b):
    mu = jnp.mean(y, axis=-1, keepdims=True)
    yc = y - mu
    var = jnp.mean(jnp.square(yc), axis=-1, keepdims=True)
    return yc * lax.rsqrt(var + LN_EPS) * g + b


def _mlp_body(at_ref, lr_ref, x_ref, g1_ref, sh2_ref, sc2_ref, g2_ref, wo_ref, l1g_ref, l1b_ref,
              wu_ref, wd_ref, l2g_ref, l2b_ref, o_ref, *, alpha, ff_chunk):
    aw = at_ref.shape[2]
    mix = jnp.dot(at_ref[0], wo_ref[0:aw, :], preferred_element_type=F32)
    mix = mix + jnp.dot(lr_ref[0], wo_ref[aw:, :], preferred_element_type=F32)
    x1 = _layer_norm(alpha * x_ref[0] + (1.0 + g1_ref[0]) * mix, l1g_ref[...], l1b_ref[...])
    h2 = (x1 * (1.0 + sc2_ref[0]) + sh2_ref[0]).astype(BF16)
    d_ff = wu_ref.shape[1]
    ff = jnp.zeros(x1.shape, F32)
    for c in range(d_ff // ff_chunk):
        cs = slice(c * ff_chunk, (c + 1) * ff_chunk)
        up = jnp.dot(h2, wu_ref[:, cs], preferred_element_type=F32)
        act = jnp.square(jnp.maximum(up, 0.0)).astype(BF16)
        ff = ff + jnp.dot(act, wd_ref[cs, :], preferred_element_type=F32)
    o_ref[0] = _layer_norm(alpha * x1 + (1.0 + g2_ref[0]) * ff, l2g_ref[...], l2b_ref[...])


def _mlp_call(attn_n, lru_n, x, mod3, w_out, ln1_g, ln1_b, w_up, w_down, ln2_g, ln2_b, alpha):
    B, S, D = x.shape
    tm = min(ROW_TILE, S)
    d_ff = w_up.shape[1]
    half = pl.BlockSpec((1, tm, attn_n.shape[2]), lambda b, i: (b, i, 0))
    rows = pl.BlockSpec((1, tm, D), lambda b, i: (b, i, 0))
    modv = lambda j: pl.BlockSpec((1, 1, D), lambda b, i: (b, 0, j))
    vec = pl.BlockSpec((1, D), lambda b, i: (0, 0))
    wspec = lambda shape: pl.BlockSpec(shape, lambda b, i: (0, 0), pipeline_mode=pl.Buffered(1))
    return pl.pallas_call(
        functools.partial(_mlp_body, alpha=alpha, ff_chunk=min(1024, d_ff)),
        grid=(B, S // tm),
        in_specs=[half, half, rows, modv(2), modv(3), modv(4), modv(5),
                  wspec((D, D)), vec, vec, wspec((D, d_ff)), wspec((d_ff, D)), vec, vec],
        out_specs=rows,
        out_shape=jax.ShapeDtypeStruct((B, S, D), F32),
        compiler_params=_cparams(2),
        name="out_mlp",
    )(attn_n, lru_n, x, mod3, mod3, mod3, mod3, w_out, ln1_g.reshape(1, D), ln1_b.reshape(1, D),
      w_up, w_down, ln2_g.reshape(1, D), ln2_b.reshape(1, D))


def _prep_w_in(w):
    D = w.shape[0]
    o_ki = 3 * ATTN_WIDTH + IDX_WIDTH
    o_xr = o_ki + IDX_HEAD_DIM + IDX_HEADS
    pad = IN_COLS_PAD - (o_xr + 2 * LRU_WIDTH)
    return jnp.concatenate([w[:, :o_ki], w[:, o_xr:], w[:, o_ki:o_xr], jnp.zeros((D, pad), w.dtype)],
                           axis=1).astype(BF16)


def _block_diag(w):
    n, c, d = w.shape
    eye = jnp.eye(n, dtype=w.dtype)
    return (w[:, :, None, :] * eye[:, None, :, None]).reshape(n * c, n * d).astype(BF16)


def kernel(x, c, positions, w_mod, b_mod, w_in, conv_w, conv_b, w_gate_a, b_gate_a, w_gate_x, b_gate_x,
           lru_lambda, g_attn_out, g_lru_out, w_out, ln1_g, ln1_b, w_up, w_down, ln2_g, ln2_b):
    B, S, D = x.shape
    L = w_mod.shape[0]
    assert S % KEY_CHUNK == 0 and S % min(ROW_TILE, S) == 0 and D % LANES == 0
    assert w_in.shape[2] == 3 * ATTN_WIDTH + IDX_WIDTH + IDX_HEAD_DIM + IDX_HEADS + 2 * LRU_WIDTH
    alpha = (2.0 * L) ** 0.25

    mod = _mod_call(c, w_mod, b_mod)
    tabs = _rope_call(positions)
    for l in range(L):
        mod3 = mod[l].reshape(B, 1, N_MOD * D)
        qT, k, vT, qiT, kia, kib, wiT, xr, yg = _inproj_call(x, mod3, _prep_w_in(w_in[l]), tabs)
        attn_n = _dsa_call(qT, qiT, wiT, k, vT, kia, kib, g_attn_out[l])
        lru_n = _lru_call(xr, yg, conv_w[l], conv_b[l], _block_diag(w_gate_a[l]), _block_diag(w_gate_x[l]),
                          b_gate_a[l], b_gate_x[l], lru_lambda[l], g_lru_out[l])
        x = _mlp_call(attn_n, lru_n, x, mod3, w_out[l].astype(BF16), ln1_g[l], ln1_b[l],
                      w_up[l].astype(BF16), w_down[l].astype(BF16), ln2_g[l], ln2_b[l], alpha)
    return x
```

```python
import functools
import math

import jax
import jax.numpy as jnp
import numpy as np
from jax import lax
from jax.experimental import pallas as pl
from jax.experimental.pallas import tpu as pltpu

ATTN_HEAD_DIM = 128
ATTN_HEADS = 4
ATTN_WIDTH = ATTN_HEADS * ATTN_HEAD_DIM
IDX_HEADS = 8
IDX_HEAD_DIM = 64
IDX_WIDTH = IDX_HEADS * IDX_HEAD_DIM
TOPK_MAX = 256
LRU_WIDTH = 512
LRU_BLOCKS = 8
LRU_BLOCK = LRU_WIDTH // LRU_BLOCKS
CONV_WIDTH = 4
LRU_C = 8.0
ROPE_THETA = 10000.0
LN_EPS = 1e-5
N_MOD = 6

LANES = 128
SUBLANES = 8
MXU_COLS = 256
VMEM_LIMIT_BYTES = 56 * 1024 * 1024

KEY_CHUNK = 256
ROW_TILE = 512
IDX_SUB = 128
ATTN_SLAB = 64
SWEEP_GROUP = 4

SEG_Q, SEG_K, SEG_V, SEG_QI, SEG_XR, SEG_YG, SEG_KIW = 0, 512, 1024, 1536, 2048, 2560, 3072
IN_COLS_PAD = 3200

BF16 = jnp.bfloat16
F32 = jnp.float32
INT_MIN = -2 ** 31
KEY_NEG_INF = int(np.int32(np.uint32(0xFF800000) ^ np.uint32(0x7FFFFFFF)))


def _cparams(n_grid):
    return pltpu.CompilerParams(
        dimension_semantics=("arbitrary",) * n_grid,
        vmem_limit_bytes=VMEM_LIMIT_BYTES)


def _mod_body(c_ref, w_ref, b_ref, o_ref):
    c = c_ref[...]
    ca = (c * jax.nn.sigmoid(c)).astype(BF16)
    o_ref[0] = jnp.dot(ca, w_ref[0].astype(BF16), preferred_element_type=F32) + b_ref[0]


def _mod_call(c, w_mod, b_mod):
    L, D, N = w_mod.shape
    B = c.shape[0]
    tn = 2048 if N % 2048 == 0 else N
    return pl.pallas_call(
        _mod_body,
        grid=(L, N // tn),
        in_specs=[pl.BlockSpec((B, D), lambda l, j: (0, 0)),
                  pl.BlockSpec((1, D, tn), lambda l, j: (l, 0, j)),
                  pl.BlockSpec((1, 1, tn), lambda l, j: (l, 0, j))],
        out_specs=pl.BlockSpec((1, B, tn), lambda l, j: (l, 0, j)),
        out_shape=jax.ShapeDtypeStruct((L, B, N), F32),
        compiler_params=_cparams(2),
        name="adaln_mod",
    )(c, w_mod, b_mod.reshape(L, 1, N))


def _rope_body(pos_ref, inv_a_ref, sgn_a_ref, inv_i_ref, sgn_i_ref, ca_ref, sa_ref, ci_ref, si_ref):
    pos = pos_ref[0].astype(F32)
    ang_a = inv_a_ref[...] * pos
    ang_i = inv_i_ref[...] * pos
    ca_ref[0] = jnp.cos(ang_a).T
    sa_ref[0] = (jnp.sin(ang_a) * sgn_a_ref[...]).T
    ci_ref[0] = jnp.cos(ang_i).T
    si_ref[0] = (jnp.sin(ang_i) * sgn_i_ref[...]).T


def _rope_call(positions):
    B, S = positions.shape
    ts = min(ROW_TILE, S)
    half_a, half_i = ATTN_HEAD_DIM // 2, IDX_HEAD_DIM // 2
    inv_a = ROPE_THETA ** (-jnp.arange(0, ATTN_HEAD_DIM, 2, dtype=F32) / ATTN_HEAD_DIM)
    inv_i = ROPE_THETA ** (-jnp.arange(0, IDX_HEAD_DIM, 2, dtype=F32) / IDX_HEAD_DIM)
    inv_a = jnp.tile(inv_a, LANES // half_a).reshape(LANES, 1)
    inv_i = jnp.tile(inv_i, LANES // half_i).reshape(LANES, 1)
    lane = np.arange(LANES)
    sgn_a = jnp.asarray(np.where(lane % ATTN_HEAD_DIM < half_a, -1.0, 1.0).astype(np.float32)).reshape(LANES, 1)
    sgn_i = jnp.asarray(np.where(lane % IDX_HEAD_DIM < half_i, -1.0, 1.0).astype(np.float32)).reshape(LANES, 1)
    col = pl.BlockSpec((LANES, 1), lambda b, i: (0, 0))
    tab = pl.BlockSpec((1, ts, LANES), lambda b, i: (b, i, 0))
    shp = jax.ShapeDtypeStruct((B, S, LANES), F32)
    return pl.pallas_call(
        _rope_body,
        grid=(B, S // ts),
        in_specs=[pl.BlockSpec((1, 1, ts), lambda b, i: (b, 0, i)), col, col, col, col],
        out_specs=[tab, tab, tab, tab],
        out_shape=[shp, shp, shp, shp],
        compiler_params=_cparams(2),
        name="rope_tables",
    )(positions.reshape(B, 1, S), inv_a, sgn_a, inv_i, sgn_i)


def _rope_a(x, cos, sin_signed):
    return x * cos + pltpu.roll(x, ATTN_HEAD_DIM // 2, 1) * sin_signed


def _rope_i(x, cos, sin_signed, lo_half):
    half = IDX_HEAD_DIM // 2
    partner = jnp.where(lo_half, pltpu.roll(x, LANES - half, 1), pltpu.roll(x, half, 1))
    return x * cos + partner * sin_signed


def _inproj_body(x_ref, sh_ref, sc_ref, w_ref, ca_ref, sa_ref, ci_ref, si_ref,
                 qT_ref, k_ref, vT_ref, qiT_ref, kia_ref, kib_ref, wiT_ref, xr_ref, yg_ref):
    tm = x_ref.shape[1]
    n_chunk = tm // KEY_CHUNK
    h = (x_ref[0] * (1.0 + sc_ref[0]) + sh_ref[0]).astype(BF16)
    ca, sa, ci, si = ca_ref[0], sa_ref[0], ci_ref[0], si_ref[0]
    lane = lax.broadcasted_iota(jnp.int32, (tm, LANES), 1)
    lo_half = (lane % IDX_HEAD_DIM) < (IDX_HEAD_DIM // 2)
    q_scale = ATTN_HEAD_DIM ** -0.5 * math.log2(math.e)

    def seg(start, width=LANES):
        return jnp.dot(h, w_ref[:, start:start + width], preferred_element_type=F32)

    def store_T(ref, g, val):
        vt = val.T.astype(BF16)
        for c in range(n_chunk):
            ref[0, c, g * LANES:(g + 1) * LANES, :] = vt[:, c * KEY_CHUNK:(c + 1) * KEY_CHUNK]

    def lane_groups(start):
        for pair in range(ATTN_WIDTH // MXU_COLS):
            wide = seg(start + pair * MXU_COLS, MXU_COLS)
            for half in range(MXU_COLS // LANES):
                yield pair * (MXU_COLS // LANES) + half, wide[:, half * LANES:(half + 1) * LANES]

    for g, val in lane_groups(SEG_Q):
        store_T(qT_ref, g, _rope_a(val, ca, sa) * q_scale)
    for g, val in lane_groups(SEG_K):
        k_ref[0, :, g * LANES:(g + 1) * LANES] = _rope_a(val, ca, sa).astype(BF16)
    for g, val in lane_groups(SEG_V):
        store_T(vT_ref, g, val)
    for g, val in lane_groups(SEG_QI):
        store_T(qiT_ref, g, _rope_i(val, ci, si, lo_half))
    xr_ref[0] = seg(SEG_XR, LRU_WIDTH)
    yg_ref[0] = seg(SEG_YG, LRU_WIDTH)
    kiw = seg(SEG_KIW)
    ki = jnp.where(lane < IDX_HEAD_DIM, _rope_i(kiw, ci, si, lo_half), 0.0)
    kia_ref[0] = ki.astype(BF16)
    kib_ref[0] = pltpu.roll(ki, IDX_HEAD_DIM, 1).astype(BF16)
    wiT = kiw.T[IDX_HEAD_DIM:IDX_HEAD_DIM + IDX_HEADS, :] * (IDX_HEADS ** -0.5)
    for c in range(n_chunk):
        wiT_ref[0, c] = wiT[:, c * KEY_CHUNK:(c + 1) * KEY_CHUNK]


def _inproj_call(x, mod3, w_in_p, tabs):
    B, S, D = x.shape
    tm = min(ROW_TILE, S)
    nc = S // KEY_CHUNK
    cpb = tm // KEY_CHUNK
    ca, sa, ci, si = tabs
    tab = pl.BlockSpec((1, tm, LANES), lambda b, i: (b, i, 0))
    tspec = pl.BlockSpec((1, cpb, ATTN_WIDTH, KEY_CHUNK), lambda b, i: (b, i, 0, 0))
    tshape = jax.ShapeDtypeStruct((B, nc, ATTN_WIDTH, KEY_CHUNK), BF16)
    rows = lambda w, dt: (pl.BlockSpec((1, tm, w), lambda b, i: (b, i, 0)), jax.ShapeDtypeStruct((B, S, w), dt))
    k_spec, k_shape = rows(ATTN_WIDTH, BF16)
    ki_spec, ki_shape = rows(LANES, BF16)
    r_spec, r_shape = rows(LRU_WIDTH, F32)
    return pl.pallas_call(
        _inproj_body,
        grid=(B, S // tm),
        in_specs=[pl.BlockSpec((1, tm, D), lambda b, i: (b, i, 0)),
                  pl.BlockSpec((1, 1, D), lambda b, i: (b, 0, 0)),
                  pl.BlockSpec((1, 1, D), lambda b, i: (b, 0, 1)),
                  pl.BlockSpec((D, IN_COLS_PAD), lambda b, i: (0, 0), pipeline_mode=pl.Buffered(1)),
                  tab, tab, tab, tab],
        out_specs=[tspec, k_spec, tspec, tspec, ki_spec, ki_spec,
                   pl.BlockSpec((1, cpb, IDX_HEADS, KEY_CHUNK), lambda b, i: (b, i, 0, 0)),
                   r_spec, r_spec],
        out_shape=[tshape, k_shape, tshape, tshape, ki_shape, ki_shape,
                   jax.ShapeDtypeStruct((B, nc, IDX_HEADS, KEY_CHUNK), F32),
                   r_shape, r_shape],
        compiler_params=_cparams(2),
        name="in_proj",
    )(x, mod3, mod3, w_in_p, ca, sa, ci, si)


_BUTTERFLY_MASK = {16: 0x0000FFFF, 8: 0x00FF00FF, 4: 0x0F0F0F0F, 2: 0x33333333, 1: 0x55555555}


def _butterfly(lo, hi, dist):
    t = (lo ^ lax.shift_right_logical(hi, jnp.int32(dist))) & jnp.int32(_BUTTERFLY_MASK[dist])
    return lo ^ t, hi ^ jnp.left_shift(t, jnp.int32(dist))


def _bit_transpose32(load, store):
    for k in range(8):
        a, b, c, d = load(k), load(k + 8), load(k + 16), load(k + 24)
        a, c = _butterfly(a, c, 16)
        b, d = _butterfly(b, d, 16)
        a, b = _butterfly(a, b, 8)
        c, d = _butterfly(c, d, 8)
        store(k, a), store(k + 8, b), store(k + 16, c), store(k + 24, d)
    for g in range(0, 32, 8):
        w = [store(g + x) for x in range(8)]
        for dist in (4, 2, 1):
            for x in range(8):
                if not x & dist:
                    w[x], w[x + dist] = _butterfly(w[x], w[x + dist], dist)
        for x in range(8):
            store(g + x, w[x])


def _fold_rows(x, op):
    r, c = x.shape
    x = x.reshape(r // SUBLANES, SUBLANES, c)
    return op(x, axis=0)


def _dsa_body(qT_ref, qiT_ref, wiT_ref, k_ref, vT_ref, kia_ref, kib_ref, g_ref, o_ref,
              key_s, plane_s, alive_s, bias_s, wb_s, sacc, s0_s, s1_s, p_s, m_s, l_s, acc_s,
              *, topk, idx_bits, group):
    s_s = (s0_s, s1_s)
    i = pl.program_id(1)
    tq = KEY_CHUNK
    tk = KEY_CHUNK
    n_chunks = i + 1
    q_pos = i * tq + lax.broadcasted_iota(jnp.int32, (1, tq), 1)
    slabs = [slice(r, r + ATTN_SLAB) for r in range(0, tk, ATTN_SLAB)]

    def walk_chunks(consume, produce=None):
        produce = produce or (lambda c, buf: None)
        produce(0, 0)

        def run(c, n):
            for k in range(n):
                produce(c + k + 1, (k + 1) % 2)
                consume(c + k, k % 2, False)

        def quad(t, carry):
            run(4 * t, 4)
            return carry
        lax.fori_loop(0, i // 4, quad, 0)

        @pl.when(i % 4 >= 2)
        def _():
            run((i // 4) * 4, 2)

        @pl.when(i % 2 == 0)
        def _():
            consume(i, 0, True)

        @pl.when(i % 2 == 1)
        def _():
            produce(i, 1)
            consume(i - 1, 0, False)
            consume(i, 1, True)

    for hd in range(IDX_HEADS):
        wb_s[hd] = jnp.broadcast_to(wiT_ref[0, 0, hd:hd + 1, :], (SUBLANES, tq))

    def score_keys(c, buf, diagonal):
        c0 = pl.multiple_of(c * tk, tk)
        ka = kia_ref[0, pl.ds(c0, tk), :]
        kb = kib_ref[0, pl.ds(c0, tk), :]
        for p in range(IDX_HEADS // 2):
            qp = qiT_ref[0, 0, p * LANES:(p + 1) * LANES, :]
            for which, kk in enumerate((ka, kb)):
                hd = 2 * p + which
                rel = jnp.maximum(jnp.dot(kk, qp, preferred_element_type=F32), 0.0)
                contrib = (rel.reshape(tk // SUBLANES, SUBLANES, tq) * wb_s[hd]).reshape(tk, tq)
                if hd == 0:
                    sacc[...] = contrib
                else:
                    sacc[...] += contrib
        for sl in slabs:
            acc = sacc[sl, :]
            bits = pltpu.bitcast(acc, jnp.int32)
            mono = bits ^ ((bits >> 31) & 0x7FFFFFFF)
            if diagonal:
                k_pos = c0 + sl.start + lax.broadcasted_iota(jnp.int32, (ATTN_SLAB, 1), 0)
                mono = jnp.where(k_pos <= q_pos, mono, KEY_NEG_INF)
            key_s[pl.ds(c0 + sl.start, ATTN_SLAB), :] = mono
        for lh in range(tq // LANES):
            ls = slice(lh * LANES, (lh + 1) * LANES)
            def key_word(r, ls=ls):
                return key_s[pl.ds(c0 + SUBLANES * r, SUBLANES), ls]

            def plane_word(b, val=None, ls=ls):
                if val is None:
                    return plane_s[c, b, :, ls]
                plane_s[c, b, :, ls] = val

            _bit_transpose32(key_word, plane_word)
            plane_s[c, 0, :, ls] = ~plane_s[c, 0, :, ls]

    walk_chunks(score_keys)

    n_groups = (n_chunks + group - 1) // group

    def pad_chunk(c, carry):
        plane_s[c] = jnp.zeros((32, SUBLANES, tq), jnp.int32)
        alive_s[c] = jnp.zeros((SUBLANES, tq), jnp.int32)
        return carry
    lax.fori_loop(n_chunks, n_groups * group, pad_chunk, 0)

    def sweep(j, keep_mask):
        def grp(g, part):
            for cc in range(group):
                c = g * group + cc
                alive = alive_s[c]
                if keep_mask is not None:
                    alive = alive & (plane_s[c, j - 1] ^ keep_mask)
                    alive_s[c] = alive
                part = part + lax.population_count(alive & plane_s[c, j])
            return part
        part = lax.fori_loop(0, n_groups, grp, jnp.zeros((SUBLANES, tq), jnp.int32))
        return part.sum(axis=0, keepdims=True)

    def decide(j, c1, state):
        thr_u, n_gt, n_alive = state
        ok = n_gt + c1 >= topk
        thr_u = thr_u | jnp.where(ok, jnp.left_shift(jnp.int32(1), 31 - j), 0)
        return jnp.where(ok, 0, -1), (thr_u, jnp.where(ok, n_gt, n_gt + c1), jnp.where(ok, c1, n_alive - c1))

    def init_alive(c, carry):
        alive_s[c] = jnp.full((SUBLANES, tq), -1, jnp.int32)
        return carry
    lax.fori_loop(0, n_chunks, init_alive, 0)
    zero = jnp.zeros((1, tq), jnp.int32)
    flip, state = decide(0, sweep(0, None), (zero, zero, zero + n_chunks * tk))

    def bit_step(j, carry):
        flip_prev, state = carry
        return decide(j, sweep(j, jnp.broadcast_to(flip_prev, (SUBLANES, tq))), state)

    flip_last, (thr_u, n_gt, n_alive) = lax.fori_loop(1, 32, bit_step, (flip, state))
    thr = thr_u ^ INT_MIN
    cnt_thr = n_gt + n_alive

    @pl.when(jnp.max(cnt_thr) > topk)
    def _():
        need = topk - n_gt
        def row_idx(c):
            return c * tk + lax.broadcasted_iota(jnp.int32, (tk, 1), 0)
        flip8 = jnp.broadcast_to(flip_last, (SUBLANES, tq))
        def settle(c, carry):
            alive_s[c] = alive_s[c] & (plane_s[c, 31] ^ flip8)
            return carry
        lax.fori_loop(0, n_chunks, settle, 0)
        def count_tied_below(cand):
            def grp(g, part):
                for cc in range(group):
                    c = g * group + cc
                    first_row = c * tk + lax.broadcasted_iota(jnp.int32, (SUBLANES, 1), 0)
                    n_below = jnp.clip((cand - first_row + (SUBLANES - 1)) >> 3, 0, 32)
                    top_bits = jnp.where(n_below > 0,
                                         jnp.left_shift(jnp.int32(-1), 32 - jnp.maximum(n_below, 1)), 0)
                    part = part + lax.population_count(alive_s[c] & top_bits)
                return part
            part = lax.fori_loop(0, n_groups, grp, jnp.zeros((SUBLANES, tq), jnp.int32))
            return part.sum(axis=0, keepdims=True)
        def idx_step(j, cut):
            cand = cut | jnp.left_shift(jnp.int32(1), idx_bits - 1 - j)
            return jnp.where(count_tied_below(cand) < need, cand, cut)
        cut = lax.fori_loop(0, idx_bits, idx_step, jnp.zeros((1, tq), jnp.int32))
        fix = cnt_thr > topk
        def rewrite(c, carry):
            sl = pl.ds(pl.multiple_of(c * tk, tk), tk)
            blk = key_s[sl, :]
            drop = (blk == thr) & (row_idx(c) > cut) & fix
            key_s[sl, :] = jnp.where(drop, blk - 1, blk)
            return carry
        lax.fori_loop(0, n_chunks, rewrite, 0)

    m_s[...] = jnp.full(m_s.shape, -jnp.inf, F32)
    l_s[...] = jnp.zeros(l_s.shape, F32)
    acc_s[...] = jnp.zeros(acc_s.shape, F32)

    heads = [slice(hd * ATTN_HEAD_DIM, (hd + 1) * ATTN_HEAD_DIM) for hd in range(ATTN_HEADS)]

    def logit_matmuls(c, buf):
        r0 = pl.multiple_of(c * tk, tk)
        for hd, hs in enumerate(heads):
            s_s[buf][hd] = jnp.dot(k_ref[0, pl.ds(r0, tk), hs], qT_ref[0, 0, hs, :],
                                   preferred_element_type=F32)

    def attn_chunk(c, buf, diagonal):
        r0 = pl.multiple_of(c * tk, tk)
        logits = s_s[buf]
        for sl in slabs:
            sel = key_s[pl.ds(r0 + sl.start, ATTN_SLAB), :] >= thr
            if diagonal:
                k_pos = r0 + sl.start + lax.broadcasted_iota(jnp.int32, (ATTN_SLAB, 1), 0)
                sel = sel & (k_pos <= q_pos)
            bias_s[sl, :] = jnp.where(sel, 0.0, -jnp.inf)
        m_safe, alpha = [], []
        for hd in range(ATTN_HEADS):
            m_old = m_s[hd]
            mx = None
            for sl in slabs:
                masked = logits[hd, sl, :] + bias_s[sl, :]
                logits[hd, sl, :] = masked
                fold = _fold_rows(masked, jnp.max)
                mx = fold if mx is None else jnp.maximum(mx, fold)
            m_new = jnp.maximum(m_old, mx.max(axis=0, keepdims=True))
            m_s[hd] = m_new
            m_safe.append(jnp.where(m_new == -jnp.inf, 0.0, m_new))
            alpha.append(jnp.exp2(m_old - m_safe[hd]))
        for hd in range(ATTN_HEADS):
            psum = jnp.zeros((SUBLANES, tq), F32)
            for sl in slabs:
                p = jnp.exp2(logits[hd, sl, :] - m_safe[hd])
                psum = psum + _fold_rows(p, jnp.sum)
                p_s[hd, sl, :] = p.astype(BF16)
            l_s[hd] = alpha[hd] * l_s[hd] + psum.sum(axis=0, keepdims=True)
        for hd, hs in enumerate(heads):
            acc_s[hd] = alpha[hd] * acc_s[hd] + jnp.dot(vT_ref[0, c, hs, :], p_s[hd],
                                                        preferred_element_type=F32)

    walk_chunks(attn_chunk, produce=logit_matmuls)

    outT = jnp.concatenate([acc_s[hd] / l_s[hd] for hd in range(ATTN_HEADS)], axis=0)
    out = outT.T
    ms = jnp.mean(jnp.square(out), axis=-1, keepdims=True)
    o_ref[0] = (out * lax.rsqrt(ms + LN_EPS) * g_ref[...]).astype(BF16)


def _dsa_call(qT, qiT, wiT, k, vT, kia, kib, g_attn):
    B, S, _ = k.shape
    nc = S // KEY_CHUNK
    topk = min(TOPK_MAX, S // 4)
    idx_bits = max(1, int(math.ceil(math.log2(S))))
    group = math.gcd(SWEEP_GROUP, nc)
    blk_q = pl.BlockSpec((1, 1, ATTN_WIDTH, KEY_CHUNK), lambda b, i: (b, i, 0, 0))
    full = lambda shape: pl.BlockSpec(shape, lambda b, i: (b,) + (0,) * (len(shape) - 1))
    return pl.pallas_call(
        functools.partial(_dsa_body, topk=topk, idx_bits=idx_bits, group=group),
        grid=(B, nc),
        in_specs=[blk_q, blk_q,
                  pl.BlockSpec((1, 1, IDX_HEADS, KEY_CHUNK), lambda b, i: (b, i, 0, 0)),
                  full((1, S, ATTN_WIDTH)),
                  full((1, nc, ATTN_WIDTH, KEY_CHUNK)),
                  full((1, S, LANES)), full((1, S, LANES)),
                  pl.BlockSpec((1, ATTN_WIDTH), lambda b, i: (0, 0))],
        out_specs=pl.BlockSpec((1, KEY_CHUNK, ATTN_WIDTH), lambda b, i: (b, i, 0)),
        out_shape=jax.ShapeDtypeStruct((B, S, ATTN_WIDTH), BF16),
        scratch_shapes=[pltpu.VMEM((S, KEY_CHUNK), jnp.int32),
                        pltpu.VMEM((nc, 32, SUBLANES, KEY_CHUNK), jnp.int32),
                        pltpu.VMEM((nc, SUBLANES, KEY_CHUNK), jnp.int32),
                        pltpu.VMEM((KEY_CHUNK, KEY_CHUNK), F32),
                        pltpu.VMEM((IDX_HEADS, SUBLANES, KEY_CHUNK), F32),
                        pltpu.VMEM((KEY_CHUNK, KEY_CHUNK), F32),
                        pltpu.VMEM((ATTN_HEADS, KEY_CHUNK, KEY_CHUNK), F32),
                        pltpu.VMEM((ATTN_HEADS, KEY_CHUNK, KEY_CHUNK), F32),
                        pltpu.VMEM((ATTN_HEADS, KEY_CHUNK, KEY_CHUNK), BF16),
                        pltpu.VMEM((ATTN_HEADS, 1, KEY_CHUNK), F32),
                        pltpu.VMEM((ATTN_HEADS, 1, KEY_CHUNK), F32),
                        pltpu.VMEM((ATTN_HEADS, ATTN_HEAD_DIM, KEY_CHUNK), F32)],
        compiler_params=_cparams(2),
        name="dsa_attention",
    )(qT, qiT, wiT, k, vT, kia, kib, g_attn.reshape(1, ATTN_WIDTH))


def _lru_body(xr_ref, yg_ref, cw_ref, cb_ref, wa_ref, wx_ref, ba_ref, bx_ref, lam_ref, g_ref, o_ref,
              xbuf_s, a_s, u_s, h_s, carry_s):
    ts = xr_ref.shape[1]
    W = xr_ref.shape[2]
    pad = SUBLANES

    @pl.when(pl.program_id(1) == 0)
    def _():
        xbuf_s[0:pad, :] = jnp.zeros((pad, W), F32)
        carry_s[...] = jnp.zeros(carry_s.shape, F32)

    xbuf_s[pad:pad + ts, :] = xr_ref[0]
    n_grp8 = ts // SUBLANES
    xg = xbuf_s[...].reshape(n_grp8 + 1, SUBLANES, W)
    sub = lax.broadcasted_iota(jnp.int32, (1, SUBLANES, 1), 1)
    xc = cb_ref[...] + xr_ref[0] * cw_ref[CONV_WIDTH - 1:CONV_WIDTH, :]
    for d in range(1, CONV_WIDTH):
        rot = pltpu.roll(xg, d, 1)
        tap = jnp.where(sub >= d, rot[1:], rot[:-1]).reshape(ts, W)
        xc = xc + tap * cw_ref[CONV_WIDTH - 1 - d:CONV_WIDTH - d, :]
    xbuf_s[0:pad, :] = xbuf_s[ts:ts + pad, :]

    xcb = xc.astype(BF16)
    r = jax.nn.sigmoid(jnp.dot(xcb, wa_ref[...], preferred_element_type=F32) + ba_ref[...])
    gi = jax.nn.sigmoid(jnp.dot(xcb, wx_ref[...], preferred_element_type=F32) + bx_ref[...])
    neg_lam = -lam_ref[...]
    softplus = jnp.maximum(neg_lam, 0.0) + jnp.log1p(jnp.exp(-jnp.abs(neg_lam)))
    log_a = -LRU_C * r * softplus
    a = jnp.exp(log_a)
    u = jnp.sqrt(-jnp.tanh(log_a) * (a * a + 1.0)) * (gi * xc)

    n_grp = ts // SUBLANES
    a = a.reshape(n_grp, SUBLANES, W)
    u = u.reshape(n_grp, SUBLANES, W)
    row = lax.broadcasted_iota(jnp.int32, (1, SUBLANES, 1), 1)
    d = 1
    while d < SUBLANES:
        keep = row >= d
        a_sh = jnp.where(keep, pltpu.roll(a, d, 1), 1.0)
        u_sh = jnp.where(keep, pltpu.roll(u, d, 1), 0.0)
        u = a * u_sh + u
        a = a * a_sh
        d *= 2
    a_s[...] = a.reshape(ts, W)
    u_s[...] = u.reshape(ts, W)

    last = SUBLANES - 1
    h_in = carry_s[last:SUBLANES, :]
    for g in range(n_grp):
        r = g * SUBLANES + last
        h_s[r:r + 1, :] = h_in
        h_in = a_s[r:r + 1, :] * h_in + u_s[r:r + 1, :]
    carry_s[last:SUBLANES, :] = h_in
    for g in range(n_grp):
        rows = slice(g * SUBLANES, (g + 1) * SUBLANES)
        r = g * SUBLANES + last
        h_s[rows, :] = a_s[rows, :] * h_s[r:r + 1, :] + u_s[rows, :]

    y = h_s[...] * jax.nn.gelu(yg_ref[0], approximate=True)
    ms = jnp.mean(jnp.square(y), axis=-1, keepdims=True)
    o_ref[0] = (y * lax.rsqrt(ms + LN_EPS) * g_ref[...]).astype(BF16)


def _lru_call(xr, yg, conv_w, conv_b, wa_bd, wx_bd, b_a, b_x, lam, g_lru):
    B, S, W = xr.shape
    ts = min(ROW_TILE, S)
    rows = pl.BlockSpec((1, ts, W), lambda b, i: (b, i, 0))
    vec = pl.BlockSpec((1, W), lambda b, i: (0, 0))
    mat = pl.BlockSpec((W, W), lambda b, i: (0, 0))
    return pl.pallas_call(
        _lru_body,
        grid=(B, S // ts),
        in_specs=[rows, rows, pl.BlockSpec((CONV_WIDTH, W), lambda b, i: (0, 0)), vec, mat, mat,
                  vec, vec, vec, vec],
        out_specs=rows,
        out_shape=jax.ShapeDtypeStruct((B, S, W), BF16),
        scratch_shapes=[pltpu.VMEM((ts + SUBLANES, W), F32),
                        pltpu.VMEM((ts, W), F32), pltpu.VMEM((ts, W), F32), pltpu.VMEM((ts, W), F32),
                        pltpu.VMEM((SUBLANES, W), F32)],
        compiler_params=_cparams(2),
        name="rg_lru",
    )(xr, yg, conv_w, conv_b.reshape(1, W), wa_bd, wx_bd, b_a.reshape(1, W), b_x.reshape(1, W),
      lam.reshape(1, W), g_lru.reshape(1, W))


def _layer_norm(y, g, b):
    mu = jnp.mean(y, axis=-1, keepdims=True)
    yc = y - mu
    var = jnp.mean(jnp.square(yc), axis=-1, keepdims=True)
    return yc * lax.rsqrt(var + LN_EPS) * g + b


def _mlp_body(at_ref, lr_ref, x_ref, g1_ref, sh2_ref, sc2_ref, g2_ref, wo_ref, l1g_ref, l1b_ref,
              wu_ref, wd_ref, l2g_ref, l2b_ref, o_ref, *, alpha, ff_chunk):
    aw = at_ref.shape[2]
    mix = jnp.dot(at_ref[0], wo_ref[0:aw, :], preferred_element_type=F32)
    mix = mix + jnp.dot(lr_ref[0], wo_ref[aw:, :], preferred_element_type=F32)
    x1 = _layer_norm(alpha * x_ref[0] + (1.0 + g1_ref[0]) * mix, l1g_ref[...], l1b_ref[...])
    h2 = (x1 * (1.0 + sc2_ref[0]) + sh2_ref[0]).astype(BF16)
    d_ff = wu_ref.shape[1]
    ff = jnp.zeros(x1.shape, F32)
    for c in range(d_ff // ff_chunk):
        cs = slice(c * ff_chunk, (c + 1) * ff_chunk)
        up = jnp.dot(h2, wu_ref[:, cs], preferred_element_type=F32)
        act = jnp.square(jnp.maximum(up, 0.0)).astype(BF16)
        ff = ff + jnp.dot(act, wd_ref[cs, :], preferred_element_type=F32)
    o_ref[0] = _layer_norm(alpha * x1 + (1.0 + g2_ref[0]) * ff, l2g_ref[...], l2b_ref[...])


def _mlp_call(attn_n, lru_n, x, mod3, w_out, ln1_g, ln1_b, w_up, w_down, ln2_g, ln2_b, alpha):
    B, S, D = x.shape
    tm = min(ROW_TILE, S)
    d_ff = w_up.shape[1]
    half = pl.BlockSpec((1, tm, attn_n.shape[2]), lambda b, i: (b, i, 0))
    rows = pl.BlockSpec((1, tm, D), lambda b, i: (b, i, 0))
    modv = lambda j: pl.BlockSpec((1, 1, D), lambda b, i: (b, 0, j))
    vec = pl.BlockSpec((1, D), lambda b, i: (0, 0))
    wspec = lambda shape: pl.BlockSpec(shape, lambda b, i: (0, 0), pipeline_mode=pl.Buffered(1))
    return pl.pallas_call(
        functools.partial(_mlp_body, alpha=alpha, ff_chunk=min(1024, d_ff)),
        grid=(B, S // tm),
        in_specs=[half, half, rows, modv(2), modv(3), modv(4), modv(5),
                  wspec((D, D)), vec, vec, wspec((D, d_ff)), wspec((d_ff, D)), vec, vec],
        out_specs=rows,
        out_shape=jax.ShapeDtypeStruct((B, S, D), F32),
        compiler_params=_cparams(2),
        name="out_mlp",
    )(attn_n, lru_n, x, mod3, mod3, mod3, mod3, w_out, ln1_g.reshape(1, D), ln1_b.reshape(1, D),
      w_up, w_down, ln2_g.reshape(1, D), ln2_b.reshape(1, D))


def _prep_w_in(w):
    D = w.shape[0]
    o_ki = 3 * ATTN_WIDTH + IDX_WIDTH
    o_xr = o_ki + IDX_HEAD_DIM + IDX_HEADS
    pad = IN_COLS_PAD - (o_xr + 2 * LRU_WIDTH)
    return jnp.concatenate([w[:, :o_ki], w[:, o_xr:], w[:, o_ki:o_xr], jnp.zeros((D, pad), w.dtype)],
                           axis=1).astype(BF16)


def _block_diag(w):
    n, c, d = w.shape
    eye = jnp.eye(n, dtype=w.dtype)
    return (w[:, :, None, :] * eye[:, None, :, None]).reshape(n * c, n * d).astype(BF16)


def kernel(x, c, positions, w_mod, b_mod, w_in, conv_w, conv_b, w_gate_a, b_gate_a, w_gate_x, b_gate_x,
           lru_lambda, g_attn_out, g_lru_out, w_out, ln1_g, ln1_b, w_up, w_down, ln2_g, ln2_b):
    B, S, D = x.shape
    L = w_mod.shape[0]
    assert S % KEY_CHUNK == 0 and S % min(ROW_TILE, S) == 0 and D % LANES == 0
    assert w_in.shape[2] == 3 * ATTN_WIDTH + IDX_WIDTH + IDX_HEAD_DIM + IDX_HEADS + 2 * LRU_WIDTH
    alpha = (2.0 * L) ** 0.25

    mod = _mod_call(c, w_mod, b_mod)
    tabs = _rope_call(positions)
    for l in range(L):
        mod3 = mod[l].reshape(B, 1, N_MOD * D)
        qT, k, vT, qiT, kia, kib, wiT, xr, yg = _inproj_call(x, mod3, _prep_w_in(w_in[l]), tabs)
        attn_n = _dsa_call(qT, qiT, wiT, k, vT, kia, kib, g_attn_out[l])
        lru_n = _lru_call(xr, yg, conv_w[l], conv_b[l], _block_diag(w_gate_a[l]), _block_diag(w_gate_x[l]),
                          b_gate_a[l], b_gate_x[l], lru_lambda[l], g_lru_out[l])
        x = _mlp_call(attn_n, lru_n, x, mod3, w_out[l].astype(BF16), ln1_g[l], ln1_b[l],
                      w_up[l].astype(BF16), w_down[l].astype(BF16), ln2_g[l], ln2_b[l], alpha)
    return x
```

```python
import functools
import math

import jax
import jax.numpy as jnp
import numpy as np
from jax import lax
from jax.experimental import pallas as pl
from jax.experimental.pallas import tpu as pltpu

ATTN_HEAD_DIM = 128
ATTN_HEADS = 4
ATTN_WIDTH = ATTN_HEADS * ATTN_HEAD_DIM
IDX_HEADS = 8
IDX_HEAD_DIM = 64
IDX_WIDTH = IDX_HEADS * IDX_HEAD_DIM
TOPK_MAX = 256
LRU_WIDTH = 512
LRU_BLOCKS = 8
LRU_BLOCK = LRU_WIDTH // LRU_BLOCKS
CONV_WIDTH = 4
LRU_C = 8.0
ROPE_THETA = 10000.0
LN_EPS = 1e-5
N_MOD = 6

LANES = 128
SUBLANES = 8
MXU_COLS = 256
VMEM_LIMIT_BYTES = 56 * 1024 * 1024

KEY_CHUNK = 256
ROW_TILE = 512
IDX_SUB = 128
ATTN_SLAB = 64
SWEEP_GROUP = 4

SEG_Q, SEG_K, SEG_V, SEG_QI, SEG_XR, SEG_YG, SEG_KIW = 0, 512, 1024, 1536, 2048, 2560, 3072
IN_COLS_PAD = 3200

BF16 = jnp.bfloat16
F32 = jnp.float32
INT_MIN = -2 ** 31
KEY_NEG_INF = int(np.int32(np.uint32(0xFF800000) ^ np.uint32(0x7FFFFFFF)))


def _cparams(n_grid):
    return pltpu.CompilerParams(
        dimension_semantics=("arbitrary",) * n_grid,
        vmem_limit_bytes=VMEM_LIMIT_BYTES)


def _mod_body(c_ref, w_ref, b_ref, o_ref):
    c = c_ref[...]
    ca = (c * jax.nn.sigmoid(c)).astype(BF16)
    o_ref[0] = jnp.dot(ca, w_ref[0].astype(BF16), preferred_element_type=F32) + b_ref[0]


def _mod_call(c, w_mod, b_mod):
    L, D, N = w_mod.shape
    B = c.shape[0]
    tn = 2048 if N % 2048 == 0 else N
    return pl.pallas_call(
        _mod_body,
        grid=(L, N // tn),
        in_specs=[pl.BlockSpec((B, D), lambda l, j: (0, 0)),
                  pl.BlockSpec((1, D, tn), lambda l, j: (l, 0, j)),
                  pl.BlockSpec((1, 1, tn), lambda l, j: (l, 0, j))],
        out_specs=pl.BlockSpec((1, B, tn), lambda l, j: (l, 0, j)),
        out_shape=jax.ShapeDtypeStruct((L, B, N), F32),
        compiler_params=_cparams(2),
        name="adaln_mod",
    )(c, w_mod, b_mod.reshape(L, 1, N))


def _rope_body(pos_ref, inv_a_ref, inv_i_ref, ca_ref, sa_ref, ci_ref, si_ref):
    pos = pos_ref[0].astype(F32)
    ang_a = inv_a_ref[...] * pos
    ang_i = inv_i_ref[...] * pos
    cos_a, sin_a, cos_i, sin_i = jnp.cos(ang_a), jnp.sin(ang_a), jnp.cos(ang_i), jnp.sin(ang_i)
    ca_ref[0] = jnp.concatenate([cos_a, cos_a], axis=0).T
    sa_ref[0] = jnp.concatenate([-sin_a, sin_a], axis=0).T
    ci_ref[0] = jnp.concatenate([cos_i, cos_i, cos_i, cos_i], axis=0).T
    si_ref[0] = jnp.concatenate([-sin_i, sin_i, -sin_i, sin_i], axis=0).T


def _rope_call(positions):
    B, S = positions.shape
    ts = min(ROW_TILE, S)
    half_a, half_i = ATTN_HEAD_DIM // 2, IDX_HEAD_DIM // 2
    inv_a = ROPE_THETA ** (-jnp.arange(0, ATTN_HEAD_DIM, 2, dtype=F32) / ATTN_HEAD_DIM)
    inv_i = ROPE_THETA ** (-jnp.arange(0, IDX_HEAD_DIM, 2, dtype=F32) / IDX_HEAD_DIM)
    col = lambda n: pl.BlockSpec((n, 1), lambda b, i: (0, 0))
    tab = pl.BlockSpec((1, ts, LANES), lambda b, i: (b, i, 0))
    shp = jax.ShapeDtypeStruct((B, S, LANES), F32)
    return pl.pallas_call(
        _rope_body,
        grid=(B, S // ts),
        in_specs=[pl.BlockSpec((1, 1, ts), lambda b, i: (b, 0, i)), col(half_a), col(half_i)],
        out_specs=[tab, tab, tab, tab],
        out_shape=[shp, shp, shp, shp],
        compiler_params=_cparams(2),
        name="rope_tables",
    )(positions.reshape(B, 1, S), inv_a.reshape(half_a, 1), inv_i.reshape(half_i, 1))


def _rope_a(x, cos, sin_signed):
    return x * cos + pltpu.roll(x, ATTN_HEAD_DIM // 2, 1) * sin_signed


def _rope_i(x, cos, sin_signed, lo_half):
    half = IDX_HEAD_DIM // 2
    partner = jnp.where(lo_half, pltpu.roll(x, LANES - half, 1), pltpu.roll(x, half, 1))
    return x * cos + partner * sin_signed


def _inproj_body(x_ref, sh_ref, sc_ref, w_ref, ca_ref, sa_ref, ci_ref, si_ref,
                 qT_ref, k_ref, vT_ref, qiT_ref, kia_ref, kib_ref, wiT_ref, xr_ref, yg_ref):
    tm = x_ref.shape[1]
    n_chunk = tm // KEY_CHUNK
    h = (x_ref[0] * (1.0 + sc_ref[0]) + sh_ref[0]).astype(BF16)
    ca, sa, ci, si = ca_ref[0], sa_ref[0], ci_ref[0], si_ref[0]
    lane = lax.broadcasted_iota(jnp.int32, (tm, LANES), 1)
    lo_half = (lane % IDX_HEAD_DIM) < (IDX_HEAD_DIM // 2)
    q_scale = ATTN_HEAD_DIM ** -0.5 * math.log2(math.e)

    def seg(start, width=LANES):
        return jnp.dot(h, w_ref[:, start:start + width], preferred_element_type=F32)

    def store_T(ref, g, val):
        vt = val.T.astype(BF16)
        for c in range(n_chunk):
            ref[0, c, g * LANES:(g + 1) * LANES, :] = vt[:, c * KEY_CHUNK:(c + 1) * KEY_CHUNK]

    def lane_groups(start):
        for pair in range(ATTN_WIDTH // MXU_COLS):
            wide = seg(start + pair * MXU_COLS, MXU_COLS)
            for half in range(MXU_COLS // LANES):
                yield pair * (MXU_COLS // LANES) + half, wide[:, half * LANES:(half + 1) * LANES]

    for g, val in lane_groups(SEG_Q):
        store_T(qT_ref, g, _rope_a(val, ca, sa) * q_scale)
    for g, val in lane_groups(SEG_K):
        k_ref[0, :, g * LANES:(g + 1) * LANES] = _rope_a(val, ca, sa).astype(BF16)
    for g, val in lane_groups(SEG_V):
        store_T(vT_ref, g, val)
    for g, val in lane_groups(SEG_QI):
        store_T(qiT_ref, g, _rope_i(val, ci, si, lo_half))
    xr_ref[0] = seg(SEG_XR, LRU_WIDTH)
    yg_ref[0] = seg(SEG_YG, LRU_WIDTH)
    kiw = seg(SEG_KIW)
    ki = jnp.where(lane < IDX_HEAD_DIM, _rope_i(kiw, ci, si, lo_half), 0.0)
    kia_ref[0] = ki.astype(BF16)
    kib_ref[0] = pltpu.roll(ki, IDX_HEAD_DIM, 1).astype(BF16)
    wiT = kiw.T[IDX_HEAD_DIM:IDX_HEAD_DIM + IDX_HEADS, :] * (IDX_HEADS ** -0.5)
    for c in range(n_chunk):
        wiT_ref[0, c] = wiT[:, c * KEY_CHUNK:(c + 1) * KEY_CHUNK]


def _inproj_call(x, mod4, w_in_p, tabs, layer):
    B, S, D = x.shape
    tm = min(ROW_TILE, S)
    nc = S // KEY_CHUNK
    cpb = tm // KEY_CHUNK
    ca, sa, ci, si = tabs
    tab = pl.BlockSpec((1, tm, LANES), lambda b, i: (b, i, 0))
    tspec = pl.BlockSpec((1, cpb, ATTN_WIDTH, KEY_CHUNK), lambda b, i: (b, i, 0, 0))
    tshape = jax.ShapeDtypeStruct((B, nc, ATTN_WIDTH, KEY_CHUNK), BF16)
    rows = lambda w, dt: (pl.BlockSpec((1, tm, w), lambda b, i: (b, i, 0)), jax.ShapeDtypeStruct((B, S, w), dt))
    k_spec, k_shape = rows(ATTN_WIDTH, BF16)
    ki_spec, ki_shape = rows(LANES, BF16)
    r_spec, r_shape = rows(LRU_WIDTH, F32)
    return pl.pallas_call(
        _inproj_body,
        grid=(B, S // tm),
        in_specs=[pl.BlockSpec((1, tm, D), lambda b, i: (b, i, 0)),
                  pl.BlockSpec((None, 1, 1, D), lambda b, i: (layer, b, 0, 0)),
                  pl.BlockSpec((None, 1, 1, D), lambda b, i: (layer, b, 0, 1)),
                  pl.BlockSpec((None, D, IN_COLS_PAD), lambda b, i: (layer, 0, 0),
                               pipeline_mode=pl.Buffered(1)),
                  tab, tab, tab, tab],
        out_specs=[tspec, k_spec, tspec, tspec, ki_spec, ki_spec,
                   pl.BlockSpec((1, cpb, IDX_HEADS, KEY_CHUNK), lambda b, i: (b, i, 0, 0)),
                   r_spec, r_spec],
        out_shape=[tshape, k_shape, tshape, tshape, ki_shape, ki_shape,
                   jax.ShapeDtypeStruct((B, nc, IDX_HEADS, KEY_CHUNK), F32),
                   r_shape, r_shape],
        compiler_params=_cparams(2),
        name="in_proj",
    )(x, mod4, mod4, w_in_p, ca, sa, ci, si)


_BUTTERFLY_MASK = {16: 0x0000FFFF, 8: 0x00FF00FF, 4: 0x0F0F0F0F, 2: 0x33333333, 1: 0x55555555}


def _butterfly(lo, hi, dist):
    t = (lo ^ lax.shift_right_logical(hi, jnp.int32(dist))) & jnp.int32(_BUTTERFLY_MASK[dist])
    return lo ^ t, hi ^ jnp.left_shift(t, jnp.int32(dist))


def _bit_transpose32(load, store):
    for k in range(8):
        a, b, c, d = load(k), load(k + 8), load(k + 16), load(k + 24)
        a, c = _butterfly(a, c, 16)
        b, d = _butterfly(b, d, 16)
        a, b = _butterfly(a, b, 8)
        c, d = _butterfly(c, d, 8)
        store(k, a), store(k + 8, b), store(k + 16, c), store(k + 24, d)
    for g in range(0, 32, 8):
        w = [store(g + x) for x in range(8)]
        for dist in (4, 2, 1):
            for x in range(8):
                if not x & dist:
                    w[x], w[x + dist] = _butterfly(w[x], w[x + dist], dist)
        for x in range(8):
            store(g + x, w[x])


def _fold_rows(x, op):
    r, c = x.shape
    x = x.reshape(r // SUBLANES, SUBLANES, c)
    return op(x, axis=0)


def _dsa_body(qT_ref, qiT_ref, wiT_ref, k_ref, vT_ref, kia_ref, kib_ref, g_ref, o_ref,
              key_s, plane_s, alive_s, bias_s, wb_s, sacc, s0_s, s1_s, p_s, m_s, l_s, acc_s,
              *, topk, idx_bits, group):
    s_s = (s0_s, s1_s)
    i = pl.program_id(1)
    tq = KEY_CHUNK
    tk = KEY_CHUNK
    n_chunks = i + 1
    q_pos = i * tq + lax.broadcasted_iota(jnp.int32, (1, tq), 1)
    slabs = [slice(r, r + ATTN_SLAB) for r in range(0, tk, ATTN_SLAB)]

    def walk_chunks(consume, produce=None):
        produce = produce or (lambda c, buf: None)
        produce(0, 0)

        def run(c, n):
            for k in range(n):
                produce(c + k + 1, (k + 1) % 2)
                consume(c + k, k % 2, False)

        def quad(t, carry):
            run(4 * t, 4)
            return carry
        lax.fori_loop(0, i // 4, quad, 0)

        @pl.when(i % 4 >= 2)
        def _():
            run((i // 4) * 4, 2)

        @pl.when(i % 2 == 0)
        def _():
            consume(i, 0, True)

        @pl.when(i % 2 == 1)
        def _():
            produce(i, 1)
            consume(i - 1, 0, False)
            consume(i, 1, True)

    for hd in range(IDX_HEADS):
        wb_s[hd] = jnp.broadcast_to(wiT_ref[0, 0, hd:hd + 1, :], (SUBLANES, tq))

    def score_keys(c, buf, diagonal):
        c0 = pl.multiple_of(c * tk, tk)
        ka = kia_ref[0, pl.ds(c0, tk), :]
        kb = kib_ref[0, pl.ds(c0, tk), :]
        for p in range(IDX_HEADS // 2):
            qp = qiT_ref[0, 0, p * LANES:(p + 1) * LANES, :]
            for which, kk in enumerate((ka, kb)):
                hd = 2 * p + which
                rel = jnp.maximum(jnp.dot(kk, qp, preferred_element_type=F32), 0.0)
                contrib = (rel.reshape(tk // SUBLANES, SUBLANES, tq) * wb_s[hd]).reshape(tk, tq)
                if hd == 0:
                    sacc[...] = contrib
                else:
                    sacc[...] += contrib
        for sl in slabs:
            acc = sacc[sl, :]
            bits = pltpu.bitcast(acc, jnp.int32)
            mono = bits ^ ((bits >> 31) & 0x7FFFFFFF)
            if diagonal:
                k_pos = c0 + sl.start + lax.broadcasted_iota(jnp.int32, (ATTN_SLAB, 1), 0)
                mono = jnp.where(k_pos <= q_pos, mono, KEY_NEG_INF)
            key_s[pl.ds(c0 + sl.start, ATTN_SLAB), :] = mono
        for lh in range(tq // LANES):
            ls = slice(lh * LANES, (lh + 1) * LANES)
            def key_word(r, ls=ls):
                return key_s[pl.ds(c0 + SUBLANES * r, SUBLANES), ls]

            def plane_word(b, val=None, ls=ls):
                if val is None:
                    return plane_s[c, b, :, ls]
                plane_s[c, b, :, ls] = val

            _bit_transpose32(key_word, plane_word)
            plane_s[c, 0, :, ls] = ~plane_s[c, 0, :, ls]

    walk_chunks(score_keys)

    n_groups = (n_chunks + group - 1) // group

    def pad_chunk(c, carry):
        plane_s[c] = jnp.zeros((32, SUBLANES, tq), jnp.int32)
        alive_s[c] = jnp.zeros((SUBLANES, tq), jnp.int32)
        return carry
    lax.fori_loop(n_chunks, n_groups * group, pad_chunk, 0)

    def sweep(j, keep_mask):
        def grp(g, part):
            for cc in range(group):
                c = g * group + cc
                alive = alive_s[c]
                if keep_mask is not None:
                    alive = alive & (plane_s[c, j - 1] ^ keep_mask)
                    alive_s[c] = alive
                part = part + lax.population_count(alive & plane_s[c, j])
            return part
        part = lax.fori_loop(0, n_groups, grp, jnp.zeros((SUBLANES, tq), jnp.int32))
        return part.sum(axis=0, keepdims=True)

    def decide(j, c1, state):
        thr_u, n_gt, n_alive = state
        ok = n_gt + c1 >= topk
        thr_u = thr_u | jnp.where(ok, jnp.left_shift(jnp.int32(1), 31 - j), 0)
        return jnp.where(ok, 0, -1), (thr_u, jnp.where(ok, n_gt, n_gt + c1), jnp.where(ok, c1, n_alive - c1))

    def init_alive(c, carry):
        alive_s[c] = jnp.full((SUBLANES, tq), -1, jnp.int32)
        return carry
    lax.fori_loop(0, n_chunks, init_alive, 0)
    zero = jnp.zeros((1, tq), jnp.int32)
    flip, state = decide(0, sweep(0, None), (zero, zero, zero + n_chunks * tk))

    def bit_step(j, carry):
        flip_prev, state = carry
        return decide(j, sweep(j, jnp.broadcast_to(flip_prev, (SUBLANES, tq))), state)

    flip_last, (thr_u, n_gt, n_alive) = lax.fori_loop(1, 32, bit_step, (flip, state))
    thr = thr_u ^ INT_MIN
    cnt_thr = n_gt + n_alive

    @pl.when(jnp.max(cnt_thr) > topk)
    def _():
        need = topk - n_gt
        def row_idx(c):
            return c * tk + lax.broadcasted_iota(jnp.int32, (tk, 1), 0)
        flip8 = jnp.broadcast_to(flip_last, (SUBLANES, tq))
        def settle(c, carry):
            alive_s[c] = alive_s[c] & (plane_s[c, 31] ^ flip8)
            return carry
        lax.fori_loop(0, n_chunks, settle, 0)
        def count_tied_below(cand):
            def grp(g, part):
                for cc in range(group):
                    c = g * group + cc
                    first_row = c * tk + lax.broadcasted_iota(jnp.int32, (SUBLANES, 1), 0)
                    n_below = jnp.clip((cand - first_row + (SUBLANES - 1)) >> 3, 0, 32)
                    top_bits = jnp.where(n_below > 0,
                                         jnp.left_shift(jnp.int32(-1), 32 - jnp.maximum(n_below, 1)), 0)
                    part = part + lax.population_count(alive_s[c] & top_bits)
                return part
            part = lax.fori_loop(0, n_groups, grp, jnp.zeros((SUBLANES, tq), jnp.int32))
            return part.sum(axis=0, keepdims=True)
        def idx_step(j, cut):
            cand = cut | jnp.left_shift(jnp.int32(1), idx_bits - 1 - j)
            return jnp.where(count_tied_below(cand) < need, cand, cut)
        cut = lax.fori_loop(0, idx_bits, idx_step, jnp.zeros((1, tq), jnp.int32))
        fix = cnt_thr > topk
        def rewrite(c, carry):
            sl = pl.ds(pl.multiple_of(c * tk, tk), tk)
            blk = key_s[sl, :]
            drop = (blk == thr) & (row_idx(c) > cut) & fix
            key_s[sl, :] = jnp.where(drop, blk - 1, blk)
            return carry
        lax.fori_loop(0, n_chunks, rewrite, 0)

    m_s[...] = jnp.full(m_s.shape, -jnp.inf, F32)
    l_s[...] = jnp.zeros(l_s.shape, F32)
    acc_s[...] = jnp.zeros(acc_s.shape, F32)

    heads = [slice(hd * ATTN_HEAD_DIM, (hd + 1) * ATTN_HEAD_DIM) for hd in range(ATTN_HEADS)]

    def logit_matmuls(c, buf):
        r0 = pl.multiple_of(c * tk, tk)
        for hd, hs in enumerate(heads):
            s_s[buf][hd] = jnp.dot(k_ref[0, pl.ds(r0, tk), hs], qT_ref[0, 0, hs, :],
                                   preferred_element_type=F32)

    def attn_chunk(c, buf, diagonal):
        r0 = pl.multiple_of(c * tk, tk)
        logits = s_s[buf]
        for sl in slabs:
            sel = key_s[pl.ds(r0 + sl.start, ATTN_SLAB), :] >= thr
            if diagonal:
                k_pos = r0 + sl.start + lax.broadcasted_iota(jnp.int32, (ATTN_SLAB, 1), 0)
                sel = sel & (k_pos <= q_pos)
            bias_s[sl, :] = jnp.where(sel, 0.0, -jnp.inf)
        m_safe, alpha = [], []
        for hd in range(ATTN_HEADS):
            m_old = m_s[hd]
            mx = None
            for sl in slabs:
                masked = logits[hd, sl, :] + bias_s[sl, :]
                logits[hd, sl, :] = masked
                fold = _fold_rows(masked, jnp.max)
                mx = fold if mx is None else jnp.maximum(mx, fold)
            m_new = jnp.maximum(m_old, mx.max(axis=0, keepdims=True))
            m_s[hd] = m_new
            m_safe.append(jnp.where(m_new == -jnp.inf, 0.0, m_new))
            alpha.append(jnp.exp2(m_old - m_safe[hd]))
        for hd in range(ATTN_HEADS):
            psum = jnp.zeros((SUBLANES, tq), F32)
            for sl in slabs:
                p = jnp.exp2(logits[hd, sl, :] - m_safe[hd])
                psum = psum + _fold_rows(p, jnp.sum)
                p_s[hd, sl, :] = p.astype(BF16)
            l_s[hd] = alpha[hd] * l_s[hd] + psum.sum(axis=0, keepdims=True)
        for hd, hs in enumerate(heads):
            acc_s[hd] = alpha[hd] * acc_s[hd] + jnp.dot(vT_ref[0, c, hs, :], p_s[hd],
                                                        preferred_element_type=F32)

    walk_chunks(attn_chunk, produce=logit_matmuls)

    outT = jnp.concatenate([acc_s[hd] / l_s[hd] for hd in range(ATTN_HEADS)], axis=0)
    out = outT.T
    ms = jnp.mean(jnp.square(out), axis=-1, keepdims=True)
    o_ref[0] = (out * lax.rsqrt(ms + LN_EPS) * g_ref[...]).astype(BF16)


def _dsa_call(qT, qiT, wiT, k, vT, kia, kib, g_attn, layer):
    B, S, _ = k.shape
    nc = S // KEY_CHUNK
    topk = min(TOPK_MAX, S // 4)
    idx_bits = max(1, int(math.ceil(math.log2(S))))
    group = math.gcd(SWEEP_GROUP, nc)
    blk_q = pl.BlockSpec((1, 1, ATTN_WIDTH, KEY_CHUNK), lambda b, i: (b, i, 0, 0))
    full = lambda shape: pl.BlockSpec(shape, lambda b, i: (b,) + (0,) * (len(shape) - 1))
    return pl.pallas_call(
        functools.partial(_dsa_body, topk=topk, idx_bits=idx_bits, group=group),
        grid=(B, nc),
        in_specs=[blk_q, blk_q,
                  pl.BlockSpec((1, 1, IDX_HEADS, KEY_CHUNK), lambda b, i: (b, i, 0, 0)),
                  full((1, S, ATTN_WIDTH)),
                  full((1, nc, ATTN_WIDTH, KEY_CHUNK)),
                  full((1, S, LANES)), full((1, S, LANES)),
                  pl.BlockSpec((None, 1, ATTN_WIDTH), lambda b, i: (layer, 0, 0))],
        out_specs=pl.BlockSpec((1, KEY_CHUNK, ATTN_WIDTH), lambda b, i: (b, i, 0)),
        out_shape=jax.ShapeDtypeStruct((B, S, ATTN_WIDTH), BF16),
        scratch_shapes=[pltpu.VMEM((S, KEY_CHUNK), jnp.int32),
                        pltpu.VMEM((nc, 32, SUBLANES, KEY_CHUNK), jnp.int32),
                        pltpu.VMEM((nc, SUBLANES, KEY_CHUNK), jnp.int32),
                        pltpu.VMEM((KEY_CHUNK, KEY_CHUNK), F32),
                        pltpu.VMEM((IDX_HEADS, SUBLANES, KEY_CHUNK), F32),
                        pltpu.VMEM((KEY_CHUNK, KEY_CHUNK), F32),
                        pltpu.VMEM((ATTN_HEADS, KEY_CHUNK, KEY_CHUNK), F32),
                        pltpu.VMEM((ATTN_HEADS, KEY_CHUNK, KEY_CHUNK), F32),
                        pltpu.VMEM((ATTN_HEADS, KEY_CHUNK, KEY_CHUNK), BF16),
                        pltpu.VMEM((ATTN_HEADS, 1, KEY_CHUNK), F32),
                        pltpu.VMEM((ATTN_HEADS, 1, KEY_CHUNK), F32),
                        pltpu.VMEM((ATTN_HEADS, ATTN_HEAD_DIM, KEY_CHUNK), F32)],
        compiler_params=_cparams(2),
        name="dsa_attention",
    )(qT, qiT, wiT, k, vT, kia, kib, g_attn)


def _lru_body(xr_ref, yg_ref, cw_ref, cb_ref, wa_ref, wx_ref, ba_ref, bx_ref, lam_ref, g_ref, o_ref,
              xbuf_s, a_s, u_s, h_s, carry_s):
    ts = xr_ref.shape[1]
    W = xr_ref.shape[2]
    pad = SUBLANES

    @pl.when(pl.program_id(1) == 0)
    def _():
        xbuf_s[0:pad, :] = jnp.zeros((pad, W), F32)
        carry_s[...] = jnp.zeros(carry_s.shape, F32)

    xbuf_s[pad:pad + ts, :] = xr_ref[0]
    n_grp8 = ts // SUBLANES
    xg = xbuf_s[...].reshape(n_grp8 + 1, SUBLANES, W)
    sub = lax.broadcasted_iota(jnp.int32, (1, SUBLANES, 1), 1)
    xc = cb_ref[...] + xr_ref[0] * cw_ref[CONV_WIDTH - 1:CONV_WIDTH, :]
    for d in range(1, CONV_WIDTH):
        rot = pltpu.roll(xg, d, 1)
        tap = jnp.where(sub >= d, rot[1:], rot[:-1]).reshape(ts, W)
        xc = xc + tap * cw_ref[CONV_WIDTH - 1 - d:CONV_WIDTH - d, :]
    xbuf_s[0:pad, :] = xbuf_s[ts:ts + pad, :]

    xcb = xc.astype(BF16)
    r = jax.nn.sigmoid(jnp.dot(xcb, wa_ref[...], preferred_element_type=F32) + ba_ref[...])
    gi = jax.nn.sigmoid(jnp.dot(xcb, wx_ref[...], preferred_element_type=F32) + bx_ref[...])
    neg_lam = -lam_ref[...]
    softplus = jnp.maximum(neg_lam, 0.0) + jnp.log1p(jnp.exp(-jnp.abs(neg_lam)))
    log_a = -LRU_C * r * softplus
    a = jnp.exp(log_a)
    u = jnp.sqrt(-jnp.tanh(log_a) * (a * a + 1.0)) * (gi * xc)

    n_grp = ts // SUBLANES
    a = a.reshape(n_grp, SUBLANES, W)
    u = u.reshape(n_grp, SUBLANES, W)
    row = lax.broadcasted_iota(jnp.int32, (1, SUBLANES, 1), 1)
    d = 1
    while d < SUBLANES:
        keep = row >= d
        a_sh = jnp.where(keep, pltpu.roll(a, d, 1), 1.0)
        u_sh = jnp.where(keep, pltpu.roll(u, d, 1), 0.0)
        u = a * u_sh + u
        a = a * a_sh
        d *= 2
    a_s[...] = a.reshape(ts, W)
    u_s[...] = u.reshape(ts, W)

    last = SUBLANES - 1
    h_in = carry_s[last:SUBLANES, :]
    for g in range(n_grp):
        r = g * SUBLANES + last
        h_s[r:r + 1, :] = h_in
        h_in = a_s[r:r + 1, :] * h_in + u_s[r:r + 1, :]
    carry_s[last:SUBLANES, :] = h_in
    for g in range(n_grp):
        rows = slice(g * SUBLANES, (g + 1) * SUBLANES)
        r = g * SUBLANES + last
        h_s[rows, :] = a_s[rows, :] * h_s[r:r + 1, :] + u_s[rows, :]

    y = h_s[...] * jax.nn.gelu(yg_ref[0], approximate=True)
    ms = jnp.mean(jnp.square(y), axis=-1, keepdims=True)
    o_ref[0] = (y * lax.rsqrt(ms + LN_EPS) * g_ref[...]).astype(BF16)


def _lru_call(xr, yg, conv_w, conv_b, wa_bd, wx_bd, b_a, b_x, lam, g_lru, layer):
    B, S, W = xr.shape
    ts = min(ROW_TILE, S)
    rows = pl.BlockSpec((1, ts, W), lambda b, i: (b, i, 0))
    vec = pl.BlockSpec((None, 1, W), lambda b, i: (layer, 0, 0))
    mat = pl.BlockSpec((None, W, W), lambda b, i: (layer, 0, 0))
    return pl.pallas_call(
        _lru_body,
        grid=(B, S // ts),
        in_specs=[rows, rows, pl.BlockSpec((None, CONV_WIDTH, W), lambda b, i: (layer, 0, 0)), vec, mat, mat,
                  vec, vec, vec, vec],
        out_specs=rows,
        out_shape=jax.ShapeDtypeStruct((B, S, W), BF16),
        scratch_shapes=[pltpu.VMEM((ts + SUBLANES, W), F32),
                        pltpu.VMEM((ts, W), F32), pltpu.VMEM((ts, W), F32), pltpu.VMEM((ts, W), F32),
                        pltpu.VMEM((SUBLANES, W), F32)],
        compiler_params=_cparams(2),
        name="rg_lru",
    )(xr, yg, conv_w, conv_b, wa_bd, wx_bd, b_a, b_x, lam, g_lru)


def _layer_norm(y, g, b):
    mu = jnp.mean(y, axis=-1, keepdims=True)
    yc = y - mu
    var = jnp.mean(jnp.square(yc), axis=-1, keepdims=True)
    return yc * lax.rsqrt(var + LN_EPS) * g + b


def _mlp_body(at_ref, lr_ref, x_ref, g1_ref, sh2_ref, sc2_ref, g2_ref, wo_ref, l1g_ref, l1b_ref,
              wu_ref, wd_ref, l2g_ref, l2b_ref, o_ref, *, alpha, ff_chunk):
    aw = at_ref.shape[2]
    mix = jnp.dot(at_ref[0], wo_ref[0:aw, :], preferred_element_type=F32)
    mix = mix + jnp.dot(lr_ref[0], wo_ref[aw:, :], preferred_element_type=F32)
    x1 = _layer_norm(alpha * x_ref[0] + (1.0 + g1_ref[0]) * mix, l1g_ref[...], l1b_ref[...])
    h2 = (x1 * (1.0 + sc2_ref[0]) + sh2_ref[0]).astype(BF16)
    d_ff = wu_ref.shape[1]
    ff = jnp.zeros(x1.shape, F32)
    for c in range(d_ff // ff_chunk):
        cs = slice(c * ff_chunk, (c + 1) * ff_chunk)
        up = jnp.dot(h2, wu_ref[:, cs], preferred_element_type=F32)
        act = jnp.square(jnp.maximum(up, 0.0)).astype(BF16)
        ff = ff + jnp.dot(act, wd_ref[cs, :], preferred_element_type=F32)
    o_ref[0] = _layer_norm(alpha * x1 + (1.0 + g2_ref[0]) * ff, l2g_ref[...], l2b_ref[...])


def _mlp_call(attn_n, lru_n, x, mod4, w_out, ln1_g, ln1_b, w_up, w_down, ln2_g, ln2_b, alpha, layer):
    B, S, D = x.shape
    tm = min(ROW_TILE, S)
    d_ff = w_up.shape[2]
    half = pl.BlockSpec((1, tm, attn_n.shape[2]), lambda b, i: (b, i, 0))
    rows = pl.BlockSpec((1, tm, D), lambda b, i: (b, i, 0))
    modv = lambda j: pl.BlockSpec((None, 1, 1, D), lambda b, i: (layer, b, 0, j))
    vec = pl.BlockSpec((None, 1, D), lambda b, i: (layer, 0, 0))
    wspec = lambda shape: pl.BlockSpec((None,) + shape, lambda b, i: (layer, 0, 0),
                                       pipeline_mode=pl.Buffered(1))
    return pl.pallas_call(
        functools.partial(_mlp_body, alpha=alpha, ff_chunk=min(1024, d_ff)),
        grid=(B, S // tm),
        in_specs=[half, half, rows, modv(2), modv(3), modv(4), modv(5),
                  wspec((D, D)), vec, vec, wspec((D, d_ff)), wspec((d_ff, D)), vec, vec],
        out_specs=rows,
        out_shape=jax.ShapeDtypeStruct((B, S, D), F32),
        compiler_params=_cparams(2),
        name="out_mlp",
    )(attn_n, lru_n, x, mod4, mod4, mod4, mod4, w_out, ln1_g, ln1_b, w_up, w_down, ln2_g, ln2_b)


def _prep_w_in(w):
    L, D, _ = w.shape
    o_ki = 3 * ATTN_WIDTH + IDX_WIDTH
    o_xr = o_ki + IDX_HEAD_DIM + IDX_HEADS
    pad = IN_COLS_PAD - (o_xr + 2 * LRU_WIDTH)
    return jnp.concatenate([w[..., :o_ki], w[..., o_xr:], w[..., o_ki:o_xr], jnp.zeros((L, D, pad), w.dtype)],
                           axis=-1).astype(BF16)


def _block_diag(w):
    L, n, c, d = w.shape
    eye = jnp.eye(n, dtype=w.dtype)
    return (w[:, :, :, None, :] * eye[None, :, None, :, None]).reshape(L, n * c, n * d).astype(BF16)


def kernel(x, c, positions, w_mod, b_mod, w_in, conv_w, conv_b, w_gate_a, b_gate_a, w_gate_x, b_gate_x,
           lru_lambda, g_attn_out, g_lru_out, w_out, ln1_g, ln1_b, w_up, w_down, ln2_g, ln2_b):
    B, S, D = x.shape
    L = w_mod.shape[0]
    assert S % KEY_CHUNK == 0 and S % min(ROW_TILE, S) == 0 and D % LANES == 0
    assert w_in.shape[2] == 3 * ATTN_WIDTH + IDX_WIDTH + IDX_HEAD_DIM + IDX_HEADS + 2 * LRU_WIDTH
    alpha = (2.0 * L) ** 0.25

    mod4 = _mod_call(c, w_mod, b_mod).reshape(L, B, 1, N_MOD * D)
    tabs = _rope_call(positions)
    vec = lambda p: p.reshape(L, 1, p.shape[-1])
    w_in_p = _prep_w_in(w_in)
    wa_bd, wx_bd = _block_diag(w_gate_a), _block_diag(w_gate_x)
    w_out_b, w_up_b, w_down_b = w_out.astype(BF16), w_up.astype(BF16), w_down.astype(BF16)
    for l in range(L):
        qT, k, vT, qiT, kia, kib, wiT, xr, yg = _inproj_call(x, mod4, w_in_p, tabs, l)
        attn_n = _dsa_call(qT, qiT, wiT, k, vT, kia, kib, vec(g_attn_out), l)
        lru_n = _lru_call(xr, yg, conv_w, vec(conv_b), wa_bd, wx_bd, vec(b_gate_a), vec(b_gate_x),
                          vec(lru_lambda), vec(g_lru_out), l)
        x = _mlp_call(attn_n, lru_n, x, mod4, w_out_b, vec(ln1_g), vec(ln1_b), w_up_b, w_down_b,
                      vec(ln2_g), vec(ln2_b), alpha, l)
    return x
```

```python
import functools
import math

import jax
import jax.numpy as jnp
import numpy as np
from jax import lax
from jax.experimental import pallas as pl
from jax.experimental.pallas import tpu as pltpu

ATTN_HEAD_DIM = 128
ATTN_HEADS = 4
ATTN_WIDTH = ATTN_HEADS * ATTN_HEAD_DIM
IDX_HEADS = 8
IDX_HEAD_DIM = 64
IDX_WIDTH = IDX_HEADS * IDX_HEAD_DIM
TOPK_MAX = 256
LRU_WIDTH = 512
LRU_BLOCKS = 8
LRU_BLOCK = LRU_WIDTH // LRU_BLOCKS
CONV_WIDTH = 4
LRU_C = 8.0
ROPE_THETA = 10000.0
LN_EPS = 1e-5
N_MOD = 6

LANES = 128
SUBLANES = 8
MXU_COLS = 256
VMEM_LIMIT_BYTES = 56 * 1024 * 1024

KEY_CHUNK = 256
ROW_TILE = 512
IDX_SUB = 128
ATTN_SLAB = 64
SWEEP_GROUP = 4

SEG_Q, SEG_K, SEG_V, SEG_QI, SEG_XR, SEG_YG, SEG_KIW = 0, 512, 1024, 1536, 2048, 2560, 3072
IN_COLS_PAD = 3200

BF16 = jnp.bfloat16
F32 = jnp.float32
INT_MIN = -2 ** 31
F32_TINY = float(np.finfo(np.float32).tiny)
KEY_NEG_INF = int(np.int32(np.uint32(0xFF800000) ^ np.uint32(0x7FFFFFFF)))


def _cparams(n_grid):
    return pltpu.CompilerParams(
        dimension_semantics=("arbitrary",) * n_grid,
        vmem_limit_bytes=VMEM_LIMIT_BYTES)


def _mod_body(c_ref, w_ref, b_ref, o_ref):
    c = c_ref[...]
    ca = (c * jax.nn.sigmoid(c)).astype(BF16)
    o_ref[0] = jnp.dot(ca, w_ref[0].astype(BF16), preferred_element_type=F32) + b_ref[0]


def _mod_call(c, w_mod, b_mod):
    L, D, N = w_mod.shape
    B = c.shape[0]
    tn = 2048 if N % 2048 == 0 else N
    return pl.pallas_call(
        _mod_body,
        grid=(L, N // tn),
        in_specs=[pl.BlockSpec((B, D), lambda l, j: (0, 0)),
                  pl.BlockSpec((1, D, tn), lambda l, j: (l, 0, j)),
                  pl.BlockSpec((1, 1, tn), lambda l, j: (l, 0, j))],
        out_specs=pl.BlockSpec((1, B, tn), lambda l, j: (l, 0, j)),
        out_shape=jax.ShapeDtypeStruct((L, B, N), F32),
        compiler_params=_cparams(2),
        name="adaln_mod",
    )(c, w_mod, b_mod.reshape(L, 1, N))


def _rope_body(pos_ref, inv_a_ref, inv_i_ref, ca_ref, sa_ref, ci_ref, si_ref):
    pos = pos_ref[0].astype(F32)
    ang_a = inv_a_ref[...] * pos
    ang_i = inv_i_ref[...] * pos
    cos_a, sin_a, cos_i, sin_i = jnp.cos(ang_a), jnp.sin(ang_a), jnp.cos(ang_i), jnp.sin(ang_i)
    ca_ref[0] = jnp.concatenate([cos_a, cos_a], axis=0).T
    sa_ref[0] = jnp.concatenate([-sin_a, sin_a], axis=0).T
    ci_ref[0] = jnp.concatenate([cos_i, cos_i, cos_i, cos_i], axis=0).T
    si_ref[0] = jnp.concatenate([-sin_i, sin_i, -sin_i, sin_i], axis=0).T


def _rope_call(positions):
    B, S = positions.shape
    ts = min(ROW_TILE, S)
    half_a, half_i = ATTN_HEAD_DIM // 2, IDX_HEAD_DIM // 2
    inv_a = ROPE_THETA ** (-jnp.arange(0, ATTN_HEAD_DIM, 2, dtype=F32) / ATTN_HEAD_DIM)
    inv_i = ROPE_THETA ** (-jnp.arange(0, IDX_HEAD_DIM, 2, dtype=F32) / IDX_HEAD_DIM)
    col = lambda n: pl.BlockSpec((n, 1), lambda b, i: (0, 0))
    tab = pl.BlockSpec((1, ts, LANES), lambda b, i: (b, i, 0))
    shp = jax.ShapeDtypeStruct((B, S, LANES), F32)
    return pl.pallas_call(
        _rope_body,
        grid=(B, S // ts),
        in_specs=[pl.BlockSpec((1, 1, ts), lambda b, i: (b, 0, i)), col(half_a), col(half_i)],
        out_specs=[tab, tab, tab, tab],
        out_shape=[shp, shp, shp, shp],
        compiler_params=_cparams(2),
        name="rope_tables",
    )(positions.reshape(B, 1, S), inv_a.reshape(half_a, 1), inv_i.reshape(half_i, 1))


def _rope_a(x, cos, sin_signed):
    return x * cos + pltpu.roll(x, ATTN_HEAD_DIM // 2, 1) * sin_signed


def _rope_i(x, cos, sin_signed, lo_half):
    half = IDX_HEAD_DIM // 2
    partner = jnp.where(lo_half, pltpu.roll(x, LANES - half, 1), pltpu.roll(x, half, 1))
    return x * cos + partner * sin_signed


def _inproj_body(x_ref, sh_ref, sc_ref, w_ref, ca_ref, sa_ref, ci_ref, si_ref,
                 qT_ref, k_ref, vT_ref, qiT_ref, kia_ref, kib_ref, wiT_ref, xr_ref, yg_ref):
    tm = x_ref.shape[1]
    n_chunk = tm // KEY_CHUNK
    h = (x_ref[0] * (1.0 + sc_ref[0]) + sh_ref[0]).astype(BF16)
    ca, sa, ci, si = ca_ref[0], sa_ref[0], ci_ref[0], si_ref[0]
    lane = lax.broadcasted_iota(jnp.int32, (tm, LANES), 1)
    lo_half = (lane % IDX_HEAD_DIM) < (IDX_HEAD_DIM // 2)
    q_scale = ATTN_HEAD_DIM ** -0.5 * math.log2(math.e)

    def seg(start, width=LANES):
        return jnp.dot(h, w_ref[:, start:start + width], preferred_element_type=F32)

    def store_T(ref, g, val):
        vt = val.T.astype(BF16)
        for c in range(n_chunk):
            ref[0, c, g * LANES:(g + 1) * LANES, :] = vt[:, c * KEY_CHUNK:(c + 1) * KEY_CHUNK]

    def lane_groups(start):
        for pair in range(ATTN_WIDTH // MXU_COLS):
            wide = seg(start + pair * MXU_COLS, MXU_COLS)
            for half in range(MXU_COLS // LANES):
                yield pair * (MXU_COLS // LANES) + half, wide[:, half * LANES:(half + 1) * LANES]

    for g, val in lane_groups(SEG_Q):
        store_T(qT_ref, g, _rope_a(val, ca, sa) * q_scale)
    for g, val in lane_groups(SEG_K):
        k_ref[0, :, g * LANES:(g + 1) * LANES] = _rope_a(val, ca, sa).astype(BF16)
    for g, val in lane_groups(SEG_V):
        store_T(vT_ref, g, val)
    for g, val in lane_groups(SEG_QI):
        store_T(qiT_ref, g, _rope_i(val, ci, si, lo_half))
    xr_ref[0] = seg(SEG_XR, LRU_WIDTH)
    yg_ref[0] = seg(SEG_YG, LRU_WIDTH)
    kiw = seg(SEG_KIW)
    ki = jnp.where(lane < IDX_HEAD_DIM, _rope_i(kiw, ci, si, lo_half), 0.0)
    kia_ref[0] = ki.astype(BF16)
    kib_ref[0] = pltpu.roll(ki, IDX_HEAD_DIM, 1).astype(BF16)
    wiT = kiw.T[IDX_HEAD_DIM:IDX_HEAD_DIM + IDX_HEADS, :] * (IDX_HEADS ** -0.5)
    for c in range(n_chunk):
        wiT_ref[0, c] = wiT[:, c * KEY_CHUNK:(c + 1) * KEY_CHUNK]


def _inproj_call(x, mod4, w_in_p, tabs, layer):
    B, S, D = x.shape
    tm = min(ROW_TILE, S)
    nc = S // KEY_CHUNK
    cpb = tm // KEY_CHUNK
    ca, sa, ci, si = tabs
    tab = pl.BlockSpec((1, tm, LANES), lambda b, i: (b, i, 0))
    tspec = pl.BlockSpec((1, cpb, ATTN_WIDTH, KEY_CHUNK), lambda b, i: (b, i, 0, 0))
    tshape = jax.ShapeDtypeStruct((B, nc, ATTN_WIDTH, KEY_CHUNK), BF16)
    rows = lambda w, dt: (pl.BlockSpec((1, tm, w), lambda b, i: (b, i, 0)), jax.ShapeDtypeStruct((B, S, w), dt))
    k_spec, k_shape = rows(ATTN_WIDTH, BF16)
    ki_spec, ki_shape = rows(LANES, BF16)
    r_spec, r_shape = rows(LRU_WIDTH, F32)
    return pl.pallas_call(
        _inproj_body,
        grid=(B, S // tm),
        in_specs=[pl.BlockSpec((1, tm, D), lambda b, i: (b, i, 0)),
                  pl.BlockSpec((None, 1, 1, D), lambda b, i: (layer, b, 0, 0)),
                  pl.BlockSpec((None, 1, 1, D), lambda b, i: (layer, b, 0, 1)),
                  pl.BlockSpec((None, D, IN_COLS_PAD), lambda b, i: (layer, 0, 0),
                               pipeline_mode=pl.Buffered(1)),
                  tab, tab, tab, tab],
        out_specs=[tspec, k_spec, tspec, tspec, ki_spec, ki_spec,
                   pl.BlockSpec((1, cpb, IDX_HEADS, KEY_CHUNK), lambda b, i: (b, i, 0, 0)),
                   r_spec, r_spec],
        out_shape=[tshape, k_shape, tshape, tshape, ki_shape, ki_shape,
                   jax.ShapeDtypeStruct((B, nc, IDX_HEADS, KEY_CHUNK), F32),
                   r_shape, r_shape],
        compiler_params=_cparams(2),
        name="in_proj",
    )(x, mod4, mod4, w_in_p, ca, sa, ci, si)


_BUTTERFLY_MASK = {16: 0x0000FFFF, 8: 0x00FF00FF, 4: 0x0F0F0F0F, 2: 0x33333333, 1: 0x55555555}


def _butterfly(lo, hi, dist):
    t = (lo ^ lax.shift_right_logical(hi, jnp.int32(dist))) & jnp.int32(_BUTTERFLY_MASK[dist])
    return lo ^ t, hi ^ jnp.left_shift(t, jnp.int32(dist))


def _bit_transpose32(load, store):
    for k in range(8):
        a, b, c, d = load(k), load(k + 8), load(k + 16), load(k + 24)
        a, c = _butterfly(a, c, 16)
        b, d = _butterfly(b, d, 16)
        a, b = _butterfly(a, b, 8)
        c, d = _butterfly(c, d, 8)
        store(k, a), store(k + 8, b), store(k + 16, c), store(k + 24, d)
    for g in range(0, 32, 8):
        w = [store(g + x) for x in range(8)]
        for dist in (4, 2, 1):
            for x in range(8):
                if not x & dist:
                    w[x], w[x + dist] = _butterfly(w[x], w[x + dist], dist)
        for x in range(8):
            store(g + x, w[x])


def _fold_rows(x, op):
    r, c = x.shape
    x = x.reshape(r // SUBLANES, SUBLANES, c)
    return op(x, axis=0)


def _dsa_body(qT_ref, qiT_ref, wiT_ref, k_ref, vT_ref, kia_ref, kib_ref, g_ref, o_ref,
              key_s, plane_s, alive_s, bias_s, wb_s, sacc, s0_s, s1_s, p_s, m_s, l_s, acc_s,
              *, topk, idx_bits, group):
    s_s = (s0_s, s1_s)
    i = pl.program_id(1)
    tq = KEY_CHUNK
    tk = KEY_CHUNK
    n_chunks = i + 1
    q_pos = i * tq + lax.broadcasted_iota(jnp.int32, (1, tq), 1)
    slabs = [slice(r, r + ATTN_SLAB) for r in range(0, tk, ATTN_SLAB)]

    def walk_chunks(consume, produce=None):
        produce = produce or (lambda c, buf: None)
        produce(0, 0)

        def run(c, n):
            for k in range(n):
                produce(c + k + 1, (k + 1) % 2)
                consume(c + k, k % 2, False)

        def quad(t, carry):
            run(4 * t, 4)
            return carry
        lax.fori_loop(0, i // 4, quad, 0)

        @pl.when(i % 4 >= 2)
        def _():
            run((i // 4) * 4, 2)

        @pl.when(i % 2 == 0)
        def _():
            consume(i, 0, True)

        @pl.when(i % 2 == 1)
        def _():
            produce(i, 1)
            consume(i - 1, 0, False)
            consume(i, 1, True)

    for hd in range(IDX_HEADS):
        wb_s[hd] = jnp.broadcast_to(wiT_ref[0, 0, hd:hd + 1, :], (SUBLANES, tq))

    def score_keys(c, buf, diagonal):
        c0 = pl.multiple_of(c * tk, tk)
        ka = kia_ref[0, pl.ds(c0, tk), :]
        kb = kib_ref[0, pl.ds(c0, tk), :]
        for p in range(IDX_HEADS // 2):
            qp = qiT_ref[0, 0, p * LANES:(p + 1) * LANES, :]
            for which, kk in enumerate((ka, kb)):
                hd = 2 * p + which
                rel = jnp.maximum(jnp.dot(kk, qp, preferred_element_type=F32), 0.0)
                contrib = (rel.reshape(tk // SUBLANES, SUBLANES, tq) * wb_s[hd]).reshape(tk, tq)
                if hd == 0:
                    sacc[...] = contrib
                else:
                    sacc[...] += contrib
        for sl in slabs:
            acc = sacc[sl, :]
            bits = pltpu.bitcast(acc, jnp.int32)
            mono = bits ^ ((bits >> 31) & 0x7FFFFFFF)
            if diagonal:
                k_pos = c0 + sl.start + lax.broadcasted_iota(jnp.int32, (ATTN_SLAB, 1), 0)
                mono = jnp.where(k_pos <= q_pos, mono, KEY_NEG_INF)
            key_s[pl.ds(c0 + sl.start, ATTN_SLAB), :] = mono
        for lh in range(tq // LANES):
            ls = slice(lh * LANES, (lh + 1) * LANES)
            def key_word(r, ls=ls):
                return key_s[pl.ds(c0 + SUBLANES * r, SUBLANES), ls]

            def plane_word(b, val=None, ls=ls):
                if val is None:
                    return plane_s[c, b, :, ls]
                plane_s[c, b, :, ls] = val

            _bit_transpose32(key_word, plane_word)
            plane_s[c, 0, :, ls] = ~plane_s[c, 0, :, ls]

    walk_chunks(score_keys)

    n_groups = (n_chunks + group - 1) // group

    def pad_chunk(c, carry):
        plane_s[c] = jnp.zeros((32, SUBLANES, tq), jnp.int32)
        alive_s[c] = jnp.zeros((SUBLANES, tq), jnp.int32)
        return carry
    lax.fori_loop(n_chunks, n_groups * group, pad_chunk, 0)

    def sweep(k, flips):
        def grp(g, parts):
            p11, p10, p01 = parts
            for cc in range(group):
                c = g * group + cc
                alive = alive_s[c]
                if flips is not None:
                    alive = alive & (plane_s[c, 2 * k - 2] ^ flips[0]) & (plane_s[c, 2 * k - 1] ^ flips[1])
                    alive_s[c] = alive
                lo = plane_s[c, 2 * k + 1]
                hi_set = alive & plane_s[c, 2 * k]
                both = hi_set & lo
                p11 = p11 + lax.population_count(both)
                p10 = p10 + lax.population_count(hi_set ^ both)
                p01 = p01 + lax.population_count((alive ^ hi_set) & lo)
            return p11, p10, p01
        zeros8 = jnp.zeros((SUBLANES, tq), jnp.int32)
        parts = lax.fori_loop(0, n_groups, grp, (zeros8, zeros8, zeros8))
        return [p.sum(axis=0, keepdims=True) for p in parts]

    def decide(k, counts, state):
        c11, c10, c01 = counts
        thr_u, n_gt, n_alive = state
        g1 = n_gt + c11
        g2 = g1 + c10
        g3 = g2 + c01
        t11 = g1 >= topk
        t10 = g2 >= topk
        t01 = g3 >= topk
        hi = t11 | t10
        lo = t11 | (jnp.logical_not(t10) & t01)
        n_gt = jnp.where(t11, n_gt, jnp.where(t10, g1, jnp.where(t01, g2, g3)))
        n_alive = jnp.where(t11, c11, jnp.where(t10, c10, jnp.where(t01, c01, n_alive - c11 - c10 - c01)))
        thr_u = (thr_u | jnp.where(hi, jnp.left_shift(jnp.int32(1), 31 - 2 * k), 0)
                 | jnp.where(lo, jnp.left_shift(jnp.int32(1), 30 - 2 * k), 0))
        return (jnp.where(hi, 0, -1), jnp.where(lo, 0, -1)), (thr_u, n_gt, n_alive)

    def init_alive(c, carry):
        alive_s[c] = jnp.full((SUBLANES, tq), -1, jnp.int32)
        return carry
    lax.fori_loop(0, n_chunks, init_alive, 0)
    zero = jnp.zeros((1, tq), jnp.int32)
    flips, state = decide(0, sweep(0, None), (zero, zero, zero + n_chunks * tk))

    def pair_step(k, carry):
        flips_prev, state = carry
        flips8 = [jnp.broadcast_to(f, (SUBLANES, tq)) for f in flips_prev]
        return decide(k, sweep(k, flips8), state)

    flips_last, (thr_u, n_gt, n_alive) = lax.fori_loop(1, 16, pair_step, (flips, state))
    thr = thr_u ^ INT_MIN
    cnt_thr = n_gt + n_alive

    @pl.when(jnp.max(cnt_thr) > topk)
    def _():
        need = topk - n_gt
        def row_idx(c):
            return c * tk + lax.broadcasted_iota(jnp.int32, (tk, 1), 0)
        flips8 = [jnp.broadcast_to(f, (SUBLANES, tq)) for f in flips_last]
        def settle(c, carry):
            alive_s[c] = alive_s[c] & (plane_s[c, 30] ^ flips8[0]) & (plane_s[c, 31] ^ flips8[1])
            return carry
        lax.fori_loop(0, n_chunks, settle, 0)
        def count_tied_below(cand):
            def grp(g, part):
                for cc in range(group):
                    c = g * group + cc
                    first_row = c * tk + lax.broadcasted_iota(jnp.int32, (SUBLANES, 1), 0)
                    n_below = jnp.clip((cand - first_row + (SUBLANES - 1)) >> 3, 0, 32)
                    top_bits = jnp.where(n_below > 0,
                                         jnp.left_shift(jnp.int32(-1), 32 - jnp.maximum(n_below, 1)), 0)
                    part = part + lax.population_count(alive_s[c] & top_bits)
                return part
            part = lax.fori_loop(0, n_groups, grp, jnp.zeros((SUBLANES, tq), jnp.int32))
            return part.sum(axis=0, keepdims=True)
        def idx_step(j, cut):
            cand = cut | jnp.left_shift(jnp.int32(1), idx_bits - 1 - j)
            return jnp.where(count_tied_below(cand) < need, cand, cut)
        cut = lax.fori_loop(0, idx_bits, idx_step, jnp.zeros((1, tq), jnp.int32))
        fix = cnt_thr > topk
        def rewrite(c, carry):
            sl = pl.ds(pl.multiple_of(c * tk, tk), tk)
            blk = key_s[sl, :]
            drop = (blk == thr) & (row_idx(c) > cut) & fix
            key_s[sl, :] = jnp.where(drop, blk - 1, blk)
            return carry
        lax.fori_loop(0, n_chunks, rewrite, 0)

    m_s[...] = jnp.full(m_s.shape, -jnp.inf, F32)
    l_s[...] = jnp.zeros(l_s.shape, F32)
    acc_s[...] = jnp.zeros(acc_s.shape, F32)

    heads = [slice(hd * ATTN_HEAD_DIM, (hd + 1) * ATTN_HEAD_DIM) for hd in range(ATTN_HEADS)]

    def logit_matmuls(c, buf):
        r0 = pl.multiple_of(c * tk, tk)
        for hd, hs in enumerate(heads):
            s_s[buf][hd] = jnp.dot(k_ref[0, pl.ds(r0, tk), hs], qT_ref[0, 0, hs, :],
                                   preferred_element_type=F32)

    def attn_chunk(c, buf, diagonal):
        r0 = pl.multiple_of(c * tk, tk)
        logits = s_s[buf]
        for sl in slabs:
            sel = key_s[pl.ds(r0 + sl.start, ATTN_SLAB), :] >= thr
            if diagonal:
                k_pos = r0 + sl.start + lax.broadcasted_iota(jnp.int32, (ATTN_SLAB, 1), 0)
                sel = sel & (k_pos <= q_pos)
            bias_s[sl, :] = jnp.where(sel, 0.0, -jnp.inf)
        m_safe, alpha = [], []
        for hd in range(ATTN_HEADS):
            m_old = m_s[hd]
            mx = None
            for sl in slabs:
                masked = logits[hd, sl, :] + bias_s[sl, :]
                logits[hd, sl, :] = masked
                fold = _fold_rows(masked, jnp.max)
                mx = fold if mx is None else jnp.maximum(mx, fold)
            m_new = jnp.maximum(m_old, mx.max(axis=0, keepdims=True))
            m_s[hd] = m_new
            m_safe.append(jnp.where(m_new == -jnp.inf, 0.0, m_new))
            alpha.append(jnp.exp2(m_old - m_safe[hd]))
        for hd in range(ATTN_HEADS):
            psum = jnp.zeros((SUBLANES, tq), F32)
            for sl in slabs:
                p = jnp.exp2(logits[hd, sl, :] - m_safe[hd])
                psum = psum + _fold_rows(p, jnp.sum)
                p_s[hd, sl, :] = p.astype(BF16)
            l_s[hd] = alpha[hd] * l_s[hd] + psum.sum(axis=0, keepdims=True)
        for hd, hs in enumerate(heads):
            acc_s[hd] = alpha[hd] * acc_s[hd] + jnp.dot(vT_ref[0, c, hs, :], p_s[hd],
                                                        preferred_element_type=F32)

    walk_chunks(attn_chunk, produce=logit_matmuls)

    outT = jnp.concatenate([acc_s[hd] / l_s[hd] for hd in range(ATTN_HEADS)], axis=0)
    out = outT.T
    ms = jnp.mean(jnp.square(out), axis=-1, keepdims=True)
    o_ref[0] = (out * lax.rsqrt(ms + LN_EPS) * g_ref[...]).astype(BF16)


def _dsa_call(qT, qiT, wiT, k, vT, kia, kib, g_attn, layer):
    B, S, _ = k.shape
    nc = S // KEY_CHUNK
    topk = min(TOPK_MAX, S // 4)
    idx_bits = max(1, int(math.ceil(math.log2(S))))
    group = math.gcd(SWEEP_GROUP, nc)
    blk_q = pl.BlockSpec((1, 1, ATTN_WIDTH, KEY_CHUNK), lambda b, i: (b, i, 0, 0))
    full = lambda shape: pl.BlockSpec(shape, lambda b, i: (b,) + (0,) * (len(shape) - 1))
    return pl.pallas_call(
        functools.partial(_dsa_body, topk=topk, idx_bits=idx_bits, group=group),
        grid=(B, nc),
        in_specs=[blk_q, blk_q,
                  pl.BlockSpec((1, 1, IDX_HEADS, KEY_CHUNK), lambda b, i: (b, i, 0, 0)),
                  full((1, S, ATTN_WIDTH)),
                  full((1, nc, ATTN_WIDTH, KEY_CHUNK)),
                  full((1, S, LANES)), full((1, S, LANES)),
                  pl.BlockSpec((None, 1, ATTN_WIDTH), lambda b, i: (layer, 0, 0))],
        out_specs=pl.BlockSpec((1, KEY_CHUNK, ATTN_WIDTH), lambda b, i: (b, i, 0)),
        out_shape=jax.ShapeDtypeStruct((B, S, ATTN_WIDTH), BF16),
        scratch_shapes=[pltpu.VMEM((S, KEY_CHUNK), jnp.int32),
                        pltpu.VMEM((nc, 32, SUBLANES, KEY_CHUNK), jnp.int32),
                        pltpu.VMEM((nc, SUBLANES, KEY_CHUNK), jnp.int32),
                        pltpu.VMEM((KEY_CHUNK, KEY_CHUNK), F32),
                        pltpu.VMEM((IDX_HEADS, SUBLANES, KEY_CHUNK), F32),
                        pltpu.VMEM((KEY_CHUNK, KEY_CHUNK), F32),
                        pltpu.VMEM((ATTN_HEADS, KEY_CHUNK, KEY_CHUNK), F32),
                        pltpu.VMEM((ATTN_HEADS, KEY_CHUNK, KEY_CHUNK), F32),
                        pltpu.VMEM((ATTN_HEADS, KEY_CHUNK, KEY_CHUNK), BF16),
                        pltpu.VMEM((ATTN_HEADS, 1, KEY_CHUNK), F32),
                        pltpu.VMEM((ATTN_HEADS, 1, KEY_CHUNK), F32),
                        pltpu.VMEM((ATTN_HEADS, ATTN_HEAD_DIM, KEY_CHUNK), F32)],
        compiler_params=_cparams(2),
        name="dsa_attention",
    )(qT, qiT, wiT, k, vT, kia, kib, g_attn)


def _sigmoid(x):
    return 0.5 * jnp.tanh(0.5 * x) + 0.5


def _lru_body(xr_ref, yg_ref, cw_ref, cb_ref, wa_ref, wx_ref, ba_ref, bx_ref, lam_ref, g_ref, o_ref,
              xbuf_s, a_s, u_s, h_s, carry_s):
    ts = xr_ref.shape[1]
    W = xr_ref.shape[2]
    pad = SUBLANES

    @pl.when(pl.program_id(1) == 0)
    def _():
        xbuf_s[0:pad, :] = jnp.zeros((pad, W), F32)
        carry_s[...] = jnp.zeros(carry_s.shape, F32)

    xbuf_s[pad:pad + ts, :] = xr_ref[0]
    n_grp8 = ts // SUBLANES
    xg = xbuf_s[...].reshape(n_grp8 + 1, SUBLANES, W)
    sub = lax.broadcasted_iota(jnp.int32, (1, SUBLANES, 1), 1)
    xc = cb_ref[...] + xr_ref[0] * cw_ref[CONV_WIDTH - 1:CONV_WIDTH, :]
    for d in range(1, CONV_WIDTH):
        rot = pltpu.roll(xg, d, 1)
        tap = jnp.where(sub >= d, rot[1:], rot[:-1]).reshape(ts, W)
        xc = xc + tap * cw_ref[CONV_WIDTH - 1 - d:CONV_WIDTH - d, :]
    xbuf_s[0:pad, :] = xbuf_s[ts:ts + pad, :]

    xcb = xc.astype(BF16)
    r = _sigmoid(jnp.dot(xcb, wa_ref[...], preferred_element_type=F32) + ba_ref[...])
    gi = _sigmoid(jnp.dot(xcb, wx_ref[...], preferred_element_type=F32) + bx_ref[...])
    neg_lam = -lam_ref[...]
    softplus = jnp.maximum(neg_lam, 0.0) + jnp.log1p(jnp.exp(-jnp.abs(neg_lam)))
    log_a = -LRU_C * r * softplus
    a = jnp.exp(log_a)
    one_minus_a2 = -jnp.tanh(log_a) * (a * a + 1.0)
    u = one_minus_a2 * lax.rsqrt(jnp.maximum(one_minus_a2, F32_TINY)) * (gi * xc)

    n_grp = ts // SUBLANES
    a = a.reshape(n_grp, SUBLANES, W)
    u = u.reshape(n_grp, SUBLANES, W)
    row = lax.broadcasted_iota(jnp.int32, (1, SUBLANES, 1), 1)
    d = 1
    while d < SUBLANES:
        keep = row >= d
        a_sh = jnp.where(keep, pltpu.roll(a, d, 1), 1.0)
        u_sh = jnp.where(keep, pltpu.roll(u, d, 1), 0.0)
        u = a * u_sh + u
        a = a * a_sh
        d *= 2
    a_s[...] = a.reshape(ts, W)
    u_s[...] = u.reshape(ts, W)

    last = SUBLANES - 1
    h_in = carry_s[last:SUBLANES, :]
    for g in range(n_grp):
        r = g * SUBLANES + last
        h_s[r:r + 1, :] = h_in
        h_in = a_s[r:r + 1, :] * h_in + u_s[r:r + 1, :]
    carry_s[last:SUBLANES, :] = h_in
    for g in range(n_grp):
        rows = slice(g * SUBLANES, (g + 1) * SUBLANES)
        r = g * SUBLANES + last
        h_s[rows, :] = a_s[rows, :] * h_s[r:r + 1, :] + u_s[rows, :]

    y = h_s[...] * jax.nn.gelu(yg_ref[0], approximate=True)
    ms = jnp.mean(jnp.square(y), axis=-1, keepdims=True)
    o_ref[0] = (y * lax.rsqrt(ms + LN_EPS) * g_ref[...]).astype(BF16)


def _lru_call(xr, yg, conv_w, conv_b, wa_bd, wx_bd, b_a, b_x, lam, g_lru, layer):
    B, S, W = xr.shape
    ts = min(ROW_TILE, S)
    rows = pl.BlockSpec((1, ts, W), lambda b, i: (b, i, 0))
    vec = pl.BlockSpec((None, 1, W), lambda b, i: (layer, 0, 0))
    mat = pl.BlockSpec((None, W, W), lambda b, i: (layer, 0, 0))
    return pl.pallas_call(
        _lru_body,
        grid=(B, S // ts),
        in_specs=[rows, rows, pl.BlockSpec((None, CONV_WIDTH, W), lambda b, i: (layer, 0, 0)), vec, mat, mat,
                  vec, vec, vec, vec],
        out_specs=rows,
        out_shape=jax.ShapeDtypeStruct((B, S, W), BF16),
        scratch_shapes=[pltpu.VMEM((ts + SUBLANES, W), F32),
                        pltpu.VMEM((ts, W), F32), pltpu.VMEM((ts, W), F32), pltpu.VMEM((ts, W), F32),
                        pltpu.VMEM((SUBLANES, W), F32)],
        compiler_params=_cparams(2),
        name="rg_lru",
    )(xr, yg, conv_w, conv_b, wa_bd, wx_bd, b_a, b_x, lam, g_lru)


def _layer_norm(y, g, b):
    mu = jnp.mean(y, axis=-1, keepdims=True)
    yc = y - mu
    var = jnp.mean(jnp.square(yc), axis=-1, keepdims=True)
    return yc * lax.rsqrt(var + LN_EPS) * g + b


def _mlp_body(at_ref, lr_ref, x_ref, g1_ref, sh2_ref, sc2_ref, g2_ref, wo_ref, l1g_ref, l1b_ref,
              wu_ref, wd_ref, l2g_ref, l2b_ref, o_ref, *, alpha, ff_chunk):
    aw = at_ref.shape[2]
    mix = jnp.dot(at_ref[0], wo_ref[0:aw, :], preferred_element_type=F32)
    mix = mix + jnp.dot(lr_ref[0], wo_ref[aw:, :], preferred_element_type=F32)
    x1 = _layer_norm(alpha * x_ref[0] + (1.0 + g1_ref[0]) * mix, l1g_ref[...], l1b_ref[...])
    h2 = (x1 * (1.0 + sc2_ref[0]) + sh2_ref[0]).astype(BF16)
    d_ff = wu_ref.shape[1]
    ff = jnp.zeros(x1.shape, F32)
    for c in range(d_ff // ff_chunk):
        cs = slice(c * ff_chunk, (c + 1) * ff_chunk)
        up = jnp.dot(h2, wu_ref[:, cs], preferred_element_type=F32)
        act = jnp.square(jnp.maximum(up, 0.0)).astype(BF16)
        ff = ff + jnp.dot(act, wd_ref[cs, :], preferred_element_type=F32)
    o_ref[0] = _layer_norm(alpha * x1 + (1.0 + g2_ref[0]) * ff, l2g_ref[...], l2b_ref[...])


def _mlp_call(attn_n, lru_n, x, mod4, w_out, ln1_g, ln1_b, w_up, w_down, ln2_g, ln2_b, alpha, layer):
    B, S, D = x.shape
    tm = min(ROW_TILE, S)
    d_ff = w_up.shape[2]
    half = pl.BlockSpec((1, tm, attn_n.shape[2]), lambda b, i: (b, i, 0))
    rows = pl.BlockSpec((1, tm, D), lambda b, i: (b, i, 0))
    modv = lambda j: pl.BlockSpec((None, 1, 1, D), lambda b, i: (layer, b, 0, j))
    vec = pl.BlockSpec((None, 1, D), lambda b, i: (layer, 0, 0))
    wspec = lambda shape: pl.BlockSpec((None,) + shape, lambda b, i: (layer, 0, 0),
                                       pipeline_mode=pl.Buffered(1))
    return pl.pallas_call(
        functools.partial(_mlp_body, alpha=alpha, ff_chunk=min(1024, d_ff)),
        grid=(B, S // tm),
        in_specs=[half, half, rows, modv(2), modv(3), modv(4), modv(5),
                  wspec((D, D)), vec, vec, wspec((D, d_ff)), wspec((d_ff, D)), vec, vec],
        out_specs=rows,
        out_shape=jax.ShapeDtypeStruct((B, S, D), F32),
        compiler_params=_cparams(2),
        name="out_mlp",
    )(attn_n, lru_n, x, mod4, mod4, mod4, mod4, w_out, ln1_g, ln1_b, w_up, w_down, ln2_g, ln2_b)


def _prep_w_in(w):
    L, D, _ = w.shape
    o_ki = 3 * ATTN_WIDTH + IDX_WIDTH
    o_xr = o_ki + IDX_HEAD_DIM + IDX_HEADS
    pad = IN_COLS_PAD - (o_xr + 2 * LRU_WIDTH)
    return jnp.concatenate([w[..., :o_ki], w[..., o_xr:], w[..., o_ki:o_xr], jnp.zeros((L, D, pad), w.dtype)],
                           axis=-1).astype(BF16)


def _block_diag(w):
    L, n, c, d = w.shape
    eye = jnp.eye(n, dtype=w.dtype)
    return (w[:, :, :, None, :] * eye[None, :, None, :, None]).reshape(L, n * c, n * d).astype(BF16)


def kernel(x, c, positions, w_mod, b_mod, w_in, conv_w, conv_b, w_gate_a, b_gate_a, w_gate_x, b_gate_x,
           lru_lambda, g_attn_out, g_lru_out, w_out, ln1_g, ln1_b, w_up, w_down, ln2_g, ln2_b):
    B, S, D = x.shape
    L = w_mod.shape[0]
    assert S % KEY_CHUNK == 0 and S % min(ROW_TILE, S) == 0 and D % LANES == 0
    assert w_in.shape[2] == 3 * ATTN_WIDTH + IDX_WIDTH + IDX_HEAD_DIM + IDX_HEADS + 2 * LRU_WIDTH
    alpha = (2.0 * L) ** 0.25

    mod4 = _mod_call(c, w_mod, b_mod).reshape(L, B, 1, N_MOD * D)
    tabs = _rope_call(positions)
    vec = lambda p: p.reshape(L, 1, p.shape[-1])
    w_in_p = _prep_w_in(w_in)
    wa_bd, wx_bd = _block_diag(w_gate_a), _block_diag(w_gate_x)
    w_out_b, w_up_b, w_down_b = w_out.astype(BF16), w_up.astype(BF16), w_down.astype(BF16)
    for l in range(L):
        qT, k, vT, qiT, kia, kib, wiT, xr, yg = _inproj_call(x, mod4, w_in_p, tabs, l)
        attn_n = _dsa_call(qT, qiT, wiT, k, vT, kia, kib, vec(g_attn_out), l)
        lru_n = _lru_call(xr, yg, conv_w, vec(conv_b), wa_bd, wx_bd, vec(b_gate_a), vec(b_gate_x),
                          vec(lru_lambda), vec(g_lru_out), l)
        x = _mlp_call(attn_n, lru_n, x, mod4, w_out_b, vec(ln1_g), vec(ln1_b), w_up_b, w_down_b,
                      vec(ln2_g), vec(ln2_b), alpha, l)
    return x
```

```python
import functools
import math

import jax
import jax.numpy as jnp
import numpy as np
from jax import lax
from jax.experimental import pallas as pl
from jax.experimental.pallas import tpu as pltpu

ATTN_HEAD_DIM = 128
ATTN_HEADS = 4
ATTN_WIDTH = ATTN_HEADS * ATTN_HEAD_DIM
IDX_HEADS = 8
IDX_HEAD_DIM = 64
IDX_WIDTH = IDX_HEADS * IDX_HEAD_DIM
TOPK_MAX = 256
LRU_WIDTH = 512
CONV_WIDTH = 4
LRU_C = 8.0
ROPE_THETA = 10000.0
LN_EPS = 1e-5
N_MOD = 6

LANES = 128
SUBLANES = 8
MXU_COLS = 256
VMEM_LIMIT_BYTES = 56 * 1024 * 1024

KEY_CHUNK = 256
ROW_TILE = 512
INPROJ_ROW_TILE = 1024
ATTN_SLAB = 64
SWEEP_GROUP = 4

SEG_Q, SEG_K, SEG_V, SEG_QI, SEG_XR, SEG_YG, SEG_KIW = 0, 512, 1024, 1536, 2048, 2560, 3072
IN_COLS_PAD = 3200

BF16 = jnp.bfloat16
F32 = jnp.float32
INT_MIN = -2 ** 31
F32_TINY = float(np.finfo(np.float32).tiny)


def _cparams(n_grid):
    return pltpu.CompilerParams(
        dimension_semantics=("arbitrary",) * n_grid,
        vmem_limit_bytes=VMEM_LIMIT_BYTES)


def _mod_body(c_ref, w_ref, b_ref, o_ref):
    c = c_ref[...]
    ca = (c * jax.nn.sigmoid(c)).astype(BF16)
    o_ref[0] = jnp.dot(ca, w_ref[0].astype(BF16), preferred_element_type=F32) + b_ref[0]


def _mod_call(c, w_mod, b_mod):
    L, D, N = w_mod.shape
    B = c.shape[0]
    tn = 2048 if N % 2048 == 0 else N
    return pl.pallas_call(
        _mod_body,
        grid=(L, N // tn),
        in_specs=[pl.BlockSpec((B, D), lambda l, j: (0, 0)),
                  pl.BlockSpec((1, D, tn), lambda l, j: (l, 0, j)),
                  pl.BlockSpec((1, 1, tn), lambda l, j: (l, 0, j))],
        out_specs=pl.BlockSpec((1, B, tn), lambda l, j: (l, 0, j)),
        out_shape=jax.ShapeDtypeStruct((L, B, N), F32),
        compiler_params=_cparams(2),
        name="adaln_mod",
    )(c, w_mod, b_mod.reshape(L, 1, N))


def _rope_body(pos_ref, inv_a_ref, inv_i_ref, ca_ref, sa_ref, ci_ref, si_ref):
    pos = pos_ref[0].astype(F32)
    ang_a = inv_a_ref[...] * pos
    ang_i = inv_i_ref[...] * pos
    cos_a, sin_a, cos_i, sin_i = jnp.cos(ang_a), jnp.sin(ang_a), jnp.cos(ang_i), jnp.sin(ang_i)
    ca_ref[0] = jnp.concatenate([cos_a, cos_a], axis=0).T
    sa_ref[0] = jnp.concatenate([-sin_a, sin_a], axis=0).T
    ci_ref[0] = jnp.concatenate([cos_i, cos_i, cos_i, cos_i], axis=0).T
    si_ref[0] = jnp.concatenate([-sin_i, sin_i, -sin_i, sin_i], axis=0).T


def _rope_call(positions):
    B, S = positions.shape
    ts = min(ROW_TILE, S)
    half_a, half_i = ATTN_HEAD_DIM // 2, IDX_HEAD_DIM // 2
    inv_a = ROPE_THETA ** (-jnp.arange(0, ATTN_HEAD_DIM, 2, dtype=F32) / ATTN_HEAD_DIM)
    inv_i = ROPE_THETA ** (-jnp.arange(0, IDX_HEAD_DIM, 2, dtype=F32) / IDX_HEAD_DIM)
    col = lambda n: pl.BlockSpec((n, 1), lambda b, i: (0, 0))
    tab = pl.BlockSpec((1, ts, LANES), lambda b, i: (b, i, 0))
    shp = jax.ShapeDtypeStruct((B, S, LANES), F32)
    return pl.pallas_call(
        _rope_body,
        grid=(B, S // ts),
        in_specs=[pl.BlockSpec((1, 1, ts), lambda b, i: (b, 0, i)), col(half_a), col(half_i)],
        out_specs=[tab, tab, tab, tab],
        out_shape=[shp, shp, shp, shp],
        compiler_params=_cparams(2),
        name="rope_tables",
    )(positions.reshape(B, 1, S), inv_a.reshape(half_a, 1), inv_i.reshape(half_i, 1))


def _rope_a(x, cos, sin_signed):
    return x * cos + pltpu.roll(x, ATTN_HEAD_DIM // 2, 1) * sin_signed


def _rope_i(x, cos, sin_signed, lo_half):
    half = IDX_HEAD_DIM // 2
    partner = jnp.where(lo_half, pltpu.roll(x, LANES - half, 1), pltpu.roll(x, half, 1))
    return x * cos + partner * sin_signed


def _inproj_body(x_ref, sh_ref, sc_ref, w_ref, ca_ref, sa_ref, ci_ref, si_ref,
                 qT_ref, k_ref, vT_ref, qiT_ref, kia_ref, kib_ref, wiT_ref, xr_ref, yg_ref):
    tm = x_ref.shape[1]
    n_chunk = tm // KEY_CHUNK
    h = (x_ref[0] * (1.0 + sc_ref[0]) + sh_ref[0]).astype(BF16)
    ca, sa, ci, si = ca_ref[0], sa_ref[0], ci_ref[0], si_ref[0]
    lane = lax.broadcasted_iota(jnp.int32, (tm, LANES), 1)
    lo_half = (lane % IDX_HEAD_DIM) < (IDX_HEAD_DIM // 2)
    q_scale = ATTN_HEAD_DIM ** -0.5 * math.log2(math.e)

    def seg(start, width=LANES):
        return jnp.dot(h, w_ref[:, start:start + width], preferred_element_type=F32)

    def store_T(ref, g, val):
        vt = val.T.astype(BF16)
        for c in range(n_chunk):
            ref[0, c, g * LANES:(g + 1) * LANES, :] = vt[:, c * KEY_CHUNK:(c + 1) * KEY_CHUNK]

    def lane_groups(start):
        for pair in range(ATTN_WIDTH // MXU_COLS):
            wide = seg(start + pair * MXU_COLS, MXU_COLS)
            for half in range(MXU_COLS // LANES):
                yield pair * (MXU_COLS // LANES) + half, wide[:, half * LANES:(half + 1) * LANES]

    for g, val in lane_groups(SEG_Q):
        store_T(qT_ref, g, _rope_a(val, ca, sa) * q_scale)
    for g, val in lane_groups(SEG_K):
        k_ref[0, :, g * LANES:(g + 1) * LANES] = _rope_a(val, ca, sa).astype(BF16)
    for g, val in lane_groups(SEG_V):
        store_T(vT_ref, g, val)
    for g, val in lane_groups(SEG_QI):
        store_T(qiT_ref, g, _rope_i(val, ci, si, lo_half))
    xr_ref[0] = seg(SEG_XR, LRU_WIDTH)
    yg_ref[0] = seg(SEG_YG, LRU_WIDTH)
    kiw = seg(SEG_KIW)
    ki = jnp.where(lane < IDX_HEAD_DIM, _rope_i(kiw, ci, si, lo_half), 0.0)
    kia_ref[0] = ki.astype(BF16)
    kib_ref[0] = pltpu.roll(ki, IDX_HEAD_DIM, 1).astype(BF16)
    wiT = kiw.T[IDX_HEAD_DIM:IDX_HEAD_DIM + IDX_HEADS, :] * (IDX_HEADS ** -0.5)
    for c in range(n_chunk):
        wiT_ref[0, c] = wiT[:, c * KEY_CHUNK:(c + 1) * KEY_CHUNK]


def _inproj_call(x, mod4, w_in_p, tabs, layer):
    B, S, D = x.shape
    tm = min(INPROJ_ROW_TILE, S)
    nc = S // KEY_CHUNK
    cpb = tm // KEY_CHUNK
    ca, sa, ci, si = tabs
    tab = pl.BlockSpec((1, tm, LANES), lambda b, i: (b, i, 0))
    tspec = pl.BlockSpec((1, cpb, ATTN_WIDTH, KEY_CHUNK), lambda b, i: (b, i, 0, 0))
    tshape = jax.ShapeDtypeStruct((B, nc, ATTN_WIDTH, KEY_CHUNK), BF16)
    rows = lambda w, dt: (pl.BlockSpec((1, tm, w), lambda b, i: (b, i, 0)), jax.ShapeDtypeStruct((B, S, w), dt))
    k_spec, k_shape = rows(ATTN_WIDTH, BF16)
    ki_spec, ki_shape = rows(LANES, BF16)
    r_spec, r_shape = rows(LRU_WIDTH, F32)
    return pl.pallas_call(
        _inproj_body,
        grid=(B, S // tm),
        in_specs=[pl.BlockSpec((1, tm, D), lambda b, i: (b, i, 0)),
                  pl.BlockSpec((None, 1, 1, D), lambda b, i: (layer, b, 0, 0)),
                  pl.BlockSpec((None, 1, 1, D), lambda b, i: (layer, b, 0, 1)),
                  pl.BlockSpec((None, D, IN_COLS_PAD), lambda b, i: (layer, 0, 0),
                               pipeline_mode=pl.Buffered(1)),
                  tab, tab, tab, tab],
        out_specs=[tspec, k_spec, tspec, tspec, ki_spec, ki_spec,
                   pl.BlockSpec((1, cpb, IDX_HEADS, KEY_CHUNK), lambda b, i: (b, i, 0, 0)),
                   r_spec, r_spec],
        out_shape=[tshape, k_shape, tshape, tshape, ki_shape, ki_shape,
                   jax.ShapeDtypeStruct((B, nc, IDX_HEADS, KEY_CHUNK), F32),
                   r_shape, r_shape],
        compiler_params=_cparams(2),
        name="in_proj",
    )(x, mod4, mod4, w_in_p, ca, sa, ci, si)


_BUTTERFLY_MASK = {16: 0x0000FFFF, 8: 0x00FF00FF, 4: 0x0F0F0F0F, 2: 0x33333333, 1: 0x55555555}


def _butterfly(lo, hi, dist):
    t = (lo ^ lax.shift_right_logical(hi, jnp.int32(dist))) & jnp.int32(_BUTTERFLY_MASK[dist])
    return lo ^ t, hi ^ jnp.left_shift(t, jnp.int32(dist))


def _bit_transpose32(load, store):
    for k in range(8):
        a, b, c, d = load(k), load(k + 8), load(k + 16), load(k + 24)
        a, c = _butterfly(a, c, 16)
        b, d = _butterfly(b, d, 16)
        a, b = _butterfly(a, b, 8)
        c, d = _butterfly(c, d, 8)
        store(k, a), store(k + 8, b), store(k + 16, c), store(k + 24, d)
    sign = None
    for g in range(0, 32, 8):
        w = [store(g + x) for x in range(8)]
        for dist in (4, 2, 1):
            for x in range(8):
                if not x & dist:
                    w[x], w[x + dist] = _butterfly(w[x], w[x + dist], dist)
        if g == 0:
            sign = w[0]
        for x in range(8):
            store(g + x, ~sign if g + x == 0 else w[x] ^ sign)


def _fold_rows(x, op):
    r, c = x.shape
    x = x.reshape(r // SUBLANES, SUBLANES, c)
    return op(x, axis=0)


def _dsa_body(qT_ref, qiT_ref, wiT_ref, k_ref, vT_ref, kia_ref, kib_ref, g_ref, o_ref,
              key_s, plane_s, alive_s, bias_s, wb_s, s0_s, s1_s, p_s, m_s, l_s, acc_s,
              *, topk, idx_bits, group):
    s_s = (s0_s, s1_s)
    i = pl.program_id(1)
    tq = KEY_CHUNK
    tk = KEY_CHUNK
    n_chunks = i + 1
    q_pos = i * tq + lax.broadcasted_iota(jnp.int32, (1, tq), 1)
    slabs = [slice(r, r + ATTN_SLAB) for r in range(0, tk, ATTN_SLAB)]

    def walk_chunks(consume, produce=None):
        produce = produce or (lambda c, buf: None)
        produce(0, 0)

        def run(c, n):
            for k in range(n):
                produce(c + k + 1, (k + 1) % 2)
                consume(c + k, k % 2, False)

        def quad(t, carry):
            run(4 * t, 4)
            return carry
        lax.fori_loop(0, i // 4, quad, 0)

        @pl.when(i % 4 >= 2)
        def _():
            run((i // 4) * 4, 2)

        @pl.when(i % 2 == 0)
        def _():
            consume(i, 0, True)

        @pl.when(i % 2 == 1)
        def _():
            produce(i, 1)
            consume(i - 1, 0, False)
            consume(i, 1, True)

    for hd in range(IDX_HEADS):
        wb_s[hd] = jnp.broadcast_to(wiT_ref[0, 0, hd:hd + 1, :], (SUBLANES, tq))

    def score_keys(c, buf, diagonal):
        c0 = pl.multiple_of(c * tk, tk)
        ka = kia_ref[0, pl.ds(c0, tk), :]
        kb = kib_ref[0, pl.ds(c0, tk), :]
        for p in range(IDX_HEADS // 2):
            qp = qiT_ref[0, 0, p * LANES:(p + 1) * LANES, :]
            for which, kk in enumerate((ka, kb)):
                hd = 2 * p + which
                rel = jnp.maximum(jnp.dot(kk, qp, preferred_element_type=F32), 0.0)
                contrib = (rel.reshape(tk // SUBLANES, SUBLANES, tq) * wb_s[hd]).reshape(tk, tq)
                if hd == 0:
                    key_s[pl.ds(c0, tk), :] = contrib
                else:
                    key_s[pl.ds(c0, tk), :] += contrib
        if diagonal:
            for sl in slabs:
                k_pos = c0 + sl.start + lax.broadcasted_iota(jnp.int32, (ATTN_SLAB, 1), 0)
                rows = pl.ds(c0 + sl.start, ATTN_SLAB)
                key_s[rows, :] = jnp.where(k_pos <= q_pos, key_s[rows, :], -jnp.inf)
        for lh in range(tq // LANES):
            ls = slice(lh * LANES, (lh + 1) * LANES)
            def key_word(r, ls=ls):
                return pltpu.bitcast(key_s[pl.ds(c0 + SUBLANES * r, SUBLANES), ls], jnp.int32)

            def plane_word(b, val=None, ls=ls):
                if val is None:
                    return plane_s[c, b, :, ls]
                plane_s[c, b, :, ls] = val

            _bit_transpose32(key_word, plane_word)

    walk_chunks(score_keys)

    n_groups = (n_chunks + group - 1) // group

    def pad_chunk(c, carry):
        plane_s[c] = jnp.zeros((32, SUBLANES, tq), jnp.int32)
        alive_s[c] = jnp.zeros((SUBLANES, tq), jnp.int32)
        return carry
    lax.fori_loop(n_chunks, n_groups * group, pad_chunk, 0)

    def sweep(k, flips):
        def grp(g, parts):
            p11, p10, p01 = parts
            for cc in range(group):
                c = g * group + cc
                alive = alive_s[c]
                if flips is not None:
                    alive = alive & (plane_s[c, 2 * k - 2] ^ flips[0]) & (plane_s[c, 2 * k - 1] ^ flips[1])
                    alive_s[c] = alive
                lo = plane_s[c, 2 * k + 1]
                hi_set = alive & plane_s[c, 2 * k]
                both = hi_set & lo
                p11 = p11 + lax.population_count(both)
                p10 = p10 + lax.population_count(hi_set ^ both)
                p01 = p01 + lax.population_count((alive ^ hi_set) & lo)
            return p11, p10, p01
        zeros8 = jnp.zeros((SUBLANES, tq), jnp.int32)
        parts = lax.fori_loop(0, n_groups, grp, (zeros8, zeros8, zeros8))
        return [p.sum(axis=0, keepdims=True) for p in parts]

    def decide(k, counts, state):
        c11, c10, c01 = counts
        thr_u, n_gt, n_alive = state
        g1 = n_gt + c11
        g2 = g1 + c10
        g3 = g2 + c01
        t11 = g1 >= topk
        t10 = g2 >= topk
        t01 = g3 >= topk
        hi = t11 | t10
        lo = t11 | (jnp.logical_not(t10) & t01)
        n_gt = jnp.where(t11, n_gt, jnp.where(t10, g1, jnp.where(t01, g2, g3)))
        n_alive = jnp.where(t11, c11, jnp.where(t10, c10, jnp.where(t01, c01, n_alive - c11 - c10 - c01)))
        thr_u = (thr_u | jnp.where(hi, jnp.left_shift(jnp.int32(1), 31 - 2 * k), 0)
                 | jnp.where(lo, jnp.left_shift(jnp.int32(1), 30 - 2 * k), 0))
        return (jnp.where(hi, 0, -1), jnp.where(lo, 0, -1)), (thr_u, n_gt, n_alive)

    def init_alive(c, carry):
        alive_s[c] = jnp.full((SUBLANES, tq), -1, jnp.int32)
        return carry
    lax.fori_loop(0, n_chunks, init_alive, 0)
    zero = jnp.zeros((1, tq), jnp.int32)
    flips, state = decide(0, sweep(0, None), (zero, zero, zero + n_chunks * tk))

    def pair_step(k, carry):
        flips_prev, state = carry
        flips8 = [jnp.broadcast_to(f, (SUBLANES, tq)) for f in flips_prev]
        return decide(k, sweep(k, flips8), state)

    flips_last, (thr_u, n_gt, n_alive) = lax.fori_loop(1, 16, pair_step, (flips, state))
    thr_bits = jnp.where(thr_u < 0, thr_u ^ INT_MIN, ~thr_u)
    thr = jnp.where(thr_u == 0, -jnp.inf, pltpu.bitcast(thr_bits, F32))
    cnt_thr = n_gt + n_alive

    @pl.when(jnp.max(cnt_thr) > topk)
    def _():
        need = topk - n_gt
        def row_idx(c):
            return c * tk + lax.broadcasted_iota(jnp.int32, (tk, 1), 0)
        flips8 = [jnp.broadcast_to(f, (SUBLANES, tq)) for f in flips_last]
        def settle(c, carry):
            alive_s[c] = alive_s[c] & (plane_s[c, 30] ^ flips8[0]) & (plane_s[c, 31] ^ flips8[1])
            return carry
        lax.fori_loop(0, n_chunks, settle, 0)
        def count_tied_below(cand):
            def grp(g, part):
                for cc in range(group):
                    c = g * group + cc
                    first_row = c * tk + lax.broadcasted_iota(jnp.int32, (SUBLANES, 1), 0)
                    n_below = jnp.clip((cand - first_row + (SUBLANES - 1)) >> 3, 0, 32)
                    top_bits = jnp.where(n_below > 0,
                                         jnp.left_shift(jnp.int32(-1), 32 - jnp.maximum(n_below, 1)), 0)
                    part = part + lax.population_count(alive_s[c] & top_bits)
                return part
            part = lax.fori_loop(0, n_groups, grp, jnp.zeros((SUBLANES, tq), jnp.int32))
            return part.sum(axis=0, keepdims=True)
        def idx_step(j, cut):
            cand = cut | jnp.left_shift(jnp.int32(1), idx_bits - 1 - j)
            return jnp.where(count_tied_below(cand) < need, cand, cut)
        cut = lax.fori_loop(0, idx_bits, idx_step, jnp.zeros((1, tq), jnp.int32))
        fix = cnt_thr > topk
        def rewrite(c, carry):
            sl = pl.ds(pl.multiple_of(c * tk, tk), tk)
            blk = key_s[sl, :]
            drop = (blk == thr) & (row_idx(c) > cut) & fix
            key_s[sl, :] = jnp.where(drop, -jnp.inf, blk)
            return carry
        lax.fori_loop(0, n_chunks, rewrite, 0)

    m_s[...] = jnp.full(m_s.shape, -jnp.inf, F32)
    l_s[...] = jnp.zeros(l_s.shape, F32)
    acc_s[...] = jnp.zeros(acc_s.shape, F32)

    heads = [slice(hd * ATTN_HEAD_DIM, (hd + 1) * ATTN_HEAD_DIM) for hd in range(ATTN_HEADS)]

    def logit_matmuls(c, buf):
        r0 = pl.multiple_of(c * tk, tk)
        for hd, hs in enumerate(heads):
            s_s[buf][hd] = jnp.dot(k_ref[0, pl.ds(r0, tk), hs], qT_ref[0, 0, hs, :],
                                   preferred_element_type=F32)

    def attn_chunk(c, buf, diagonal):
        r0 = pl.multiple_of(c * tk, tk)
        logits = s_s[buf]
        for sl in slabs:
            sel = key_s[pl.ds(r0 + sl.start, ATTN_SLAB), :] >= thr
            if diagonal:
                k_pos = r0 + sl.start + lax.broadcasted_iota(jnp.int32, (ATTN_SLAB, 1), 0)
                sel = sel & (k_pos <= q_pos)
            bias_s[sl, :] = jnp.where(sel, 0.0, -jnp.inf)
        m_safe, alpha = [], []
        for hd in range(ATTN_HEADS):
            m_old = m_s[hd]
            mx = None
            for sl in slabs:
                masked = logits[hd, sl, :] + bias_s[sl, :]
                logits[hd, sl, :] = masked
                fold = _fold_rows(masked, jnp.max)
                mx = fold if mx is None else jnp.maximum(mx, fold)
            m_new = jnp.maximum(m_old, mx.max(axis=0, keepdims=True))
            m_s[hd] = m_new
            m_safe.append(jnp.where(m_new == -jnp.inf, 0.0, m_new))
            alpha.append(jnp.exp2(m_old - m_safe[hd]))
        for hd in range(ATTN_HEADS):
            psum = jnp.zeros((SUBLANES, tq), F32)
            for sl in slabs:
                p = jnp.exp2(logits[hd, sl, :] - m_safe[hd])
                psum = psum + _fold_rows(p, jnp.sum)
                p_s[hd, sl, :] = p.astype(BF16)
            l_s[hd] = alpha[hd] * l_s[hd] + psum.sum(axis=0, keepdims=True)
        for hd, hs in enumerate(heads):
            acc_s[hd] = alpha[hd] * acc_s[hd] + jnp.dot(vT_ref[0, c, hs, :], p_s[hd],
                                                        preferred_element_type=F32)

    walk_chunks(attn_chunk, produce=logit_matmuls)

    outT = jnp.concatenate([acc_s[hd] / l_s[hd] for hd in range(ATTN_HEADS)], axis=0)
    out = outT.T
    ms = jnp.mean(jnp.square(out), axis=-1, keepdims=True)
    o_ref[0] = (out * lax.rsqrt(ms + LN_EPS) * g_ref[...]).astype(BF16)


def _dsa_call(qT, qiT, wiT, k, vT, kia, kib, g_attn, layer):
    B, S, _ = k.shape
    nc = S // KEY_CHUNK
    topk = min(TOPK_MAX, S // 4)
    idx_bits = max(1, int(math.ceil(math.log2(S))))
    group = math.gcd(SWEEP_GROUP, nc)
    blk_q = pl.BlockSpec((1, 1, ATTN_WIDTH, KEY_CHUNK), lambda b, i: (b, i, 0, 0))
    full = lambda shape: pl.BlockSpec(shape, lambda b, i: (b,) + (0,) * (len(shape) - 1))
    return pl.pallas_call(
        functools.partial(_dsa_body, topk=topk, idx_bits=idx_bits, group=group),
        grid=(B, nc),
        in_specs=[blk_q, blk_q,
                  pl.BlockSpec((1, 1, IDX_HEADS, KEY_CHUNK), lambda b, i: (b, i, 0, 0)),
                  full((1, S, ATTN_WIDTH)),
                  full((1, nc, ATTN_WIDTH, KEY_CHUNK)),
                  full((1, S, LANES)), full((1, S, LANES)),
                  pl.BlockSpec((None, 1, ATTN_WIDTH), lambda b, i: (layer, 0, 0))],
        out_specs=pl.BlockSpec((1, KEY_CHUNK, ATTN_WIDTH), lambda b, i: (b, i, 0)),
        out_shape=jax.ShapeDtypeStruct((B, S, ATTN_WIDTH), BF16),
        scratch_shapes=[pltpu.VMEM((S, KEY_CHUNK), F32),
                        pltpu.VMEM((nc, 32, SUBLANES, KEY_CHUNK), jnp.int32),
                        pltpu.VMEM((nc, SUBLANES, KEY_CHUNK), jnp.int32),
                        pltpu.VMEM((KEY_CHUNK, KEY_CHUNK), F32),
                        pltpu.VMEM((IDX_HEADS, SUBLANES, KEY_CHUNK), F32),
                        pltpu.VMEM((ATTN_HEADS, KEY_CHUNK, KEY_CHUNK), F32),
                        pltpu.VMEM((ATTN_HEADS, KEY_CHUNK, KEY_CHUNK), F32),
                        pltpu.VMEM((ATTN_HEADS, KEY_CHUNK, KEY_CHUNK), BF16),
                        pltpu.VMEM((ATTN_HEADS, 1, KEY_CHUNK), F32),
                        pltpu.VMEM((ATTN_HEADS, 1, KEY_CHUNK), F32),
                        pltpu.VMEM((ATTN_HEADS, ATTN_HEAD_DIM, KEY_CHUNK), F32)],
        compiler_params=_cparams(2),
        name="dsa_attention",
    )(qT, qiT, wiT, k, vT, kia, kib, g_attn)


def _sigmoid(x):
    return 0.5 * jnp.tanh(0.5 * x) + 0.5


def _lru_body(xr_ref, yg_ref, cw_ref, cb_ref, wa_ref, wx_ref, ba_ref, bx_ref, lam_ref, g_ref, o_ref,
              xbuf_s, a_s, u_s, h_s, carry_s):
    ts = xr_ref.shape[1]
    W = xr_ref.shape[2]
    pad = SUBLANES

    @pl.when(pl.program_id(1) == 0)
    def _():
        xbuf_s[0:pad, :] = jnp.zeros((pad, W), F32)
        carry_s[...] = jnp.zeros(carry_s.shape, F32)

    xbuf_s[pad:pad + ts, :] = xr_ref[0]
    n_grp8 = ts // SUBLANES
    xg = xbuf_s[...].reshape(n_grp8 + 1, SUBLANES, W)
    sub = lax.broadcasted_iota(jnp.int32, (1, SUBLANES, 1), 1)
    xc = cb_ref[...] + xr_ref[0] * cw_ref[CONV_WIDTH - 1:CONV_WIDTH, :]
    for d in range(1, CONV_WIDTH):
        rot = pltpu.roll(xg, d, 1)
        tap = jnp.where(sub >= d, rot[1:], rot[:-1]).reshape(ts, W)
        xc = xc + tap * cw_ref[CONV_WIDTH - 1 - d:CONV_WIDTH - d, :]
    xbuf_s[0:pad, :] = xbuf_s[ts:ts + pad, :]

    xcb = xc.astype(BF16)
    r = _sigmoid(jnp.dot(xcb, wa_ref[...], preferred_element_type=F32) + ba_ref[...])
    gi = _sigmoid(jnp.dot(xcb, wx_ref[...], preferred_element_type=F32) + bx_ref[...])
    neg_lam = -lam_ref[...]
    softplus = jnp.maximum(neg_lam, 0.0) + jnp.log1p(jnp.exp(-jnp.abs(neg_lam)))
    log_a = -LRU_C * r * softplus
    a = jnp.exp(log_a)
    one_minus_a2 = -jnp.tanh(log_a) * (a * a + 1.0)
    u = one_minus_a2 * lax.rsqrt(jnp.maximum(one_minus_a2, F32_TINY)) * (gi * xc)

    n_grp = ts // SUBLANES
    a = a.reshape(n_grp, SUBLANES, W)
    u = u.reshape(n_grp, SUBLANES, W)
    row = lax.broadcasted_iota(jnp.int32, (1, SUBLANES, 1), 1)
    d = 1
    while d < SUBLANES:
        keep = row >= d
        a_sh = jnp.where(keep, pltpu.roll(a, d, 1), 1.0)
        u_sh = jnp.where(keep, pltpu.roll(u, d, 1), 0.0)
        u = a * u_sh + u
        a = a * a_sh
        d *= 2
    a_s[...] = a.reshape(ts, W)
    u_s[...] = u.reshape(ts, W)

    last = SUBLANES - 1
    h_in = carry_s[last:SUBLANES, :]
    for g in range(n_grp):
        r = g * SUBLANES + last
        h_s[r:r + 1, :] = h_in
        h_in = a_s[r:r + 1, :] * h_in + u_s[r:r + 1, :]
    carry_s[last:SUBLANES, :] = h_in
    for g in range(n_grp):
        rows = slice(g * SUBLANES, (g + 1) * SUBLANES)
        r = g * SUBLANES + last
        h_s[rows, :] = a_s[rows, :] * h_s[r:r + 1, :] + u_s[rows, :]

    y = h_s[...] * jax.nn.gelu(yg_ref[0], approximate=True)
    ms = jnp.mean(jnp.square(y), axis=-1, keepdims=True)
    o_ref[0] = (y * lax.rsqrt(ms + LN_EPS) * g_ref[...]).astype(BF16)


def _lru_call(xr, yg, conv_w, conv_b, wa_bd, wx_bd, b_a, b_x, lam, g_lru, layer):
    B, S, W = xr.shape
    ts = min(ROW_TILE, S)
    rows = pl.BlockSpec((1, ts, W), lambda b, i: (b, i, 0))
    vec = pl.BlockSpec((None, 1, W), lambda b, i: (layer, 0, 0))
    mat = pl.BlockSpec((None, W, W), lambda b, i: (layer, 0, 0))
    return pl.pallas_call(
        _lru_body,
        grid=(B, S // ts),
        in_specs=[rows, rows, pl.BlockSpec((None, CONV_WIDTH, W), lambda b, i: (layer, 0, 0)), vec, mat, mat,
                  vec, vec, vec, vec],
        out_specs=rows,
        out_shape=jax.ShapeDtypeStruct((B, S, W), BF16),
        scratch_shapes=[pltpu.VMEM((ts + SUBLANES, W), F32),
                        pltpu.VMEM((ts, W), F32), pltpu.VMEM((ts, W), F32), pltpu.VMEM((ts, W), F32),
                        pltpu.VMEM((SUBLANES, W), F32)],
        compiler_params=_cparams(2),
        name="rg_lru",
    )(xr, yg, conv_w, conv_b, wa_bd, wx_bd, b_a, b_x, lam, g_lru)


def _layer_norm(y, g, b):
    mu = jnp.mean(y, axis=-1, keepdims=True)
    yc = y - mu
    var = jnp.mean(jnp.square(yc), axis=-1, keepdims=True)
    return yc * lax.rsqrt(var + LN_EPS) * g + b


def _mlp_body(at_ref, lr_ref, x_ref, g1_ref, sh2_ref, sc2_ref, g2_ref, wo_ref, l1g_ref, l1b_ref,
              wu_ref, wd_ref, l2g_ref, l2b_ref, o_ref, *, alpha, ff_chunk):
    aw = at_ref.shape[2]
    mix = jnp.dot(at_ref[0], wo_ref[0:aw, :], preferred_element_type=F32)
    mix = mix + jnp.dot(lr_ref[0], wo_ref[aw:, :], preferred_element_type=F32)
    x1 = _layer_norm(alpha * x_ref[0] + (1.0 + g1_ref[0]) * mix, l1g_ref[...], l1b_ref[...])
    h2 = (x1 * (1.0 + sc2_ref[0]) + sh2_ref[0]).astype(BF16)
    d_ff = wu_ref.shape[1]
    ff = jnp.zeros(x1.shape, F32)
    for c in range(d_ff // ff_chunk):
        cs = slice(c * ff_chunk, (c + 1) * ff_chunk)
        up = jnp.dot(h2, wu_ref[:, cs], preferred_element_type=F32)
        act = jnp.square(jnp.maximum(up, 0.0)).astype(BF16)
        ff = ff + jnp.dot(act, wd_ref[cs, :], preferred_element_type=F32)
    o_ref[0] = _layer_norm(alpha * x1 + (1.0 + g2_ref[0]) * ff, l2g_ref[...], l2b_ref[...])


def _mlp_call(attn_n, lru_n, x, mod4, w_out, ln1_g, ln1_b, w_up, w_down, ln2_g, ln2_b, alpha, layer):
    B, S, D = x.shape
    tm = min(ROW_TILE, S)
    d_ff = w_up.shape[2]
    half = pl.BlockSpec((1, tm, attn_n.shape[2]), lambda b, i: (b, i, 0))
    rows = pl.BlockSpec((1, tm, D), lambda b, i: (b, i, 0))
    modv = lambda j: pl.BlockSpec((None, 1, 1, D), lambda b, i: (layer, b, 0, j))
    vec = pl.BlockSpec((None, 1, D), lambda b, i: (layer, 0, 0))
    wspec = lambda shape: pl.BlockSpec((None,) + shape, lambda b, i: (layer, 0, 0),
                                       pipeline_mode=pl.Buffered(1))
    return pl.pallas_call(
        functools.partial(_mlp_body, alpha=alpha, ff_chunk=min(1024, d_ff)),
        grid=(B, S // tm),
        in_specs=[half, half, rows, modv(2), modv(3), modv(4), modv(5),
                  wspec((D, D)), vec, vec, wspec((D, d_ff)), wspec((d_ff, D)), vec, vec],
        out_specs=rows,
        out_shape=jax.ShapeDtypeStruct((B, S, D), F32),
        compiler_params=_cparams(2),
        name="out_mlp",
    )(attn_n, lru_n, x, mod4, mod4, mod4, mod4, w_out, ln1_g, ln1_b, w_up, w_down, ln2_g, ln2_b)


def _prep_w_in(w):
    L, D, _ = w.shape
    o_ki = 3 * ATTN_WIDTH + IDX_WIDTH
    o_xr = o_ki + IDX_HEAD_DIM + IDX_HEADS
    pad = IN_COLS_PAD - (o_xr + 2 * LRU_WIDTH)
    return jnp.concatenate([w[..., :o_ki], w[..., o_xr:], w[..., o_ki:o_xr], jnp.zeros((L, D, pad), w.dtype)],
                           axis=-1).astype(BF16)


def _block_diag(w):
    L, n, c, d = w.shape
    eye = jnp.eye(n, dtype=w.dtype)
    return (w[:, :, :, None, :] * eye[None, :, None, :, None]).reshape(L, n * c, n * d).astype(BF16)


def kernel(x, c, positions, w_mod, b_mod, w_in, conv_w, conv_b, w_gate_a, b_gate_a, w_gate_x, b_gate_x,
           lru_lambda, g_attn_out, g_lru_out, w_out, ln1_g, ln1_b, w_up, w_down, ln2_g, ln2_b):
    B, S, D = x.shape
    L = w_mod.shape[0]
    assert S % KEY_CHUNK == 0 and S % min(INPROJ_ROW_TILE, S) == 0 and D % LANES == 0
    assert w_in.shape[2] == 3 * ATTN_WIDTH + IDX_WIDTH + IDX_HEAD_DIM + IDX_HEADS + 2 * LRU_WIDTH
    alpha = (2.0 * L) ** 0.25

    mod4 = _mod_call(c, w_mod, b_mod).reshape(L, B, 1, N_MOD * D)
    tabs = _rope_call(positions)
    vec = lambda p: p.reshape(L, 1, p.shape[-1])
    w_in_p = _prep_w_in(w_in)
    wa_bd, wx_bd = _block_diag(w_gate_a), _block_diag(w_gate_x)
    w_out_b, w_up_b, w_down_b = w_out.astype(BF16), w_up.astype(BF16), w_down.astype(BF16)
    for l in range(L):
        qT, k, vT, qiT, kia, kib, wiT, xr, yg = _inproj_call(x, mod4, w_in_p, tabs, l)
        attn_n = _dsa_call(qT, qiT, wiT, k, vT, kia, kib, vec(g_attn_out), l)
        lru_n = _lru_call(xr, yg, conv_w, vec(conv_b), wa_bd, wx_bd, vec(b_gate_a), vec(b_gate_x),
                          vec(lru_lambda), vec(g_lru_out), l)
        x = _mlp_call(attn_n, lru_n, x, mod4, w_out_b, vec(ln1_g), vec(ln1_b), w_up_b, w_down_b,
                      vec(ln2_g), vec(ln2_b), alpha, l)
    return x
```
